```python
import math
import jax, jax.numpy as jnp
from jax import lax
import numpy as np

D_MODEL = 1024
BATCH = 32
SEQ = 256
DEPTH = 2
DEC_BATCH = 8
DEC_SEQ = 2048
PAST_LEN = 512

GRID_W = 64
N_MIXERS = 2
N_CONV_LAYERS = (DEPTH + 1) // 2
N_ATTN_LAYERS = DEPTH // 2
HEAD_DIM = 128
N_HEADS = D_MODEL // HEAD_DIM
N_KV_HEADS = 2
QKV_DIM = (N_HEADS + 2 * N_KV_HEADS) * HEAD_DIM
D_FF = 2816
CONV_WIDTH = 3
Q_BLOCK = 128
ROPE_THETA = 10000.0
EPS = 1e-6

kernel_name = "hybrid_diffusion_conv_gqa_step"


def rms_norm(x, g):
    xf = x.astype(jnp.float32)
    y = xf * lax.rsqrt(jnp.mean(xf * xf, axis=-1, keepdims=True) + EPS)
    return (y * g.astype(jnp.float32)).astype(x.dtype)


def dwconv3(x, w):
    xp = jnp.pad(x, ((0, 0), (1, 1), (0, 0)))
    return xp[:, :-2] * w[0] + xp[:, 1:-1] * w[1] + xp[:, 2:] * w[2]


def short_conv_mixer(u, w_in, conv_k, w_out):
    b_gate, c_gate, xp = jnp.split(u @ w_in, 3, axis=-1)
    return (b_gate * dwconv3(c_gate * xp, conv_k)) @ w_out


def conv_ffn(u, w_up, conv_k, w_down):
    h = dwconv3(u @ w_up, conv_k)
    g, up = jnp.split(h, 2, axis=-1)
    return (jax.nn.silu(g) * up) @ w_down


def qkv_heads(u, w_qkv, q_gain, k_gain):
    b, l, _ = u.shape
    qkv = u @ w_qkv
    q = qkv[..., :N_HEADS * HEAD_DIM].reshape(b, l, N_HEADS, HEAD_DIM)
    k = qkv[..., N_HEADS * HEAD_DIM:(N_HEADS + N_KV_HEADS) * HEAD_DIM].reshape(b, l, N_KV_HEADS, HEAD_DIM)
    v = qkv[..., (N_HEADS + N_KV_HEADS) * HEAD_DIM:].reshape(b, l, N_KV_HEADS, HEAD_DIM)
    return rms_norm(q, q_gain), rms_norm(k, k_gain), v


def rope_axis(xh, pos):
    d = xh.shape[-1]
    inv = ROPE_THETA ** (-jnp.arange(0, d, 2, dtype=jnp.float32) / d)
    ang = pos.astype(jnp.float32)[:, None] * inv[None, :]
    cos = jnp.cos(ang)[None, :, None, :]
    sin = jnp.sin(ang)[None, :, None, :]
    x1, x2 = jnp.split(xh.astype(jnp.float32), 2, axis=-1)
    return jnp.concatenate([x1 * cos - x2 * sin, x2 * cos + x1 * sin], axis=-1)


def rope_2d(x, row, col):
    half = HEAD_DIM // 2
    y = jnp.concatenate([rope_axis(x[..., :half], row), rope_axis(x[..., half:], col)], axis=-1)
    return y.astype(x.dtype)


def block_attention(q, k, v):
    b, lq, _, _ = q.shape
    g = N_HEADS // N_KV_HEADS
    nb = lq // Q_BLOCK
    qb = q.reshape(b, nb, Q_BLOCK, N_KV_HEADS, g, HEAD_DIM).transpose(1, 0, 2, 3, 4, 5)
    kf = k.astype(jnp.float32)
    vf = v.astype(jnp.float32)
    scale = HEAD_DIM ** -0.5

    def one_block(qblk):
        s = jnp.einsum('bqkgd,bskd->bkgqs', qblk.astype(jnp.float32), kf) * scale
        p = jax.nn.softmax(s, axis=-1)
        return jnp.einsum('bkgqs,bskd->bqkgd', p, vf).astype(q.dtype)

    o = lax.map(one_block, qb)
    return o.transpose(1, 0, 2, 3, 4, 5).reshape(b, lq, N_HEADS * HEAD_DIM)


def modulate_norm(h, g, shift, scale):
    return rms_norm(h, g) * (1.0 + scale) + shift


def gated_residual(h, y, g, gate):
    return h + gate * rms_norm(y, g)


def setup_inputs(seed: int = 0) -> dict:
    key = jax.random.key(seed)
    ks = jax.random.split(key, 24)

    def nrm(k, shape, scale):
        return jax.random.normal(k, shape, jnp.float32) * scale

    def gain(k, shape):
        return 1.0 + 0.05 * jax.random.normal(k, shape, jnp.float32)

    kv_shape_dec = (DEC_BATCH, N_ATTN_LAYERS, PAST_LEN, N_KV_HEADS, HEAD_DIM)
    return {
        'x_prompt': nrm(ks[0], (BATCH, SEQ, D_MODEL), 1.0),
        'x_sample': nrm(ks[1], (DEC_BATCH, DEC_SEQ, D_MODEL), 1.0),
        'cache_k': nrm(ks[2], kv_shape_dec, 1.0),
        'cache_v': nrm(ks[3], kv_shape_dec, 1.0),
        'c': nrm(ks[4], (DEC_BATCH, D_MODEL), 1.0),
        'c_ctx': nrm(ks[5], (D_MODEL,), 1.0),
        'mod_w': nrm(ks[6], (DEPTH, D_MODEL, 6 * D_MODEL), 0.5 * D_MODEL ** -0.5),
        'mod_b': nrm(ks[7], (DEPTH, 6 * D_MODEL), 0.02),
        'norm_mix_pre': gain(ks[8], (DEPTH, D_MODEL)),
        'norm_mix_post': gain(ks[9], (DEPTH, D_MODEL)),
        'norm_ffn_pre': gain(ks[10], (DEPTH, D_MODEL)),
        'norm_ffn_post': gain(ks[11], (DEPTH, D_MODEL)),
        'conv_w_in': nrm(ks[12], (N_CONV_LAYERS, D_MODEL, 3 * D_MODEL), D_MODEL ** -0.5),
        'conv_k': nrm(ks[13], (N_CONV_LAYERS, CONV_WIDTH, D_MODEL), 0.5),
        'conv_w_out': nrm(ks[14], (N_CONV_LAYERS, D_MODEL, D_MODEL), D_MODEL ** -0.5),
        'attn_w_qkv': nrm(ks[15], (N_ATTN_LAYERS, D_MODEL, QKV_DIM), D_MODEL ** -0.5),
        'attn_q_gain': gain(ks[16], (N_ATTN_LAYERS, HEAD_DIM)),
        'attn_k_gain': gain(ks[17], (N_ATTN_LAYERS, HEAD_DIM)),
        'attn_w_o': nrm(ks[18], (N_ATTN_LAYERS, N_HEADS * HEAD_DIM, D_MODEL), (N_HEADS * HEAD_DIM) ** -0.5),
        'ffn_w_up': nrm(ks[19], (DEPTH, D_MODEL, 2 * D_FF), D_MODEL ** -0.5),
        'ffn_conv': nrm(ks[20], (DEPTH, CONV_WIDTH, 2 * D_FF), 0.5),
        'ffn_w_down': nrm(ks[21], (DEPTH, D_FF, D_MODEL), D_FF ** -0.5),
    }


def reference(x_prompt, x_sample, cache_k, cache_v, c, c_ctx, mod_w, mod_b,
              norm_mix_pre, norm_mix_post, norm_ffn_pre, norm_ffn_post,
              conv_w_in, conv_k, conv_w_out, attn_w_qkv, attn_q_gain, attn_k_gain, attn_w_o,
              ffn_w_up, ffn_conv, ffn_w_down):
    n_lat = x_sample.shape[1]
    rows_n = n_lat // GRID_W
    row = jnp.repeat(jnp.arange(rows_n, dtype=jnp.int32), GRID_W)
    col = jnp.tile(jnp.arange(GRID_W, dtype=jnp.int32), rows_n)

    silu_c = jax.nn.silu(c)
    silu_ctx = jax.nn.silu(c_ctx)

    h_ctx = x_prompt
    h_lat = x_sample
    new_k_list = []
    new_v_list = []
    for i in range(DEPTH):
        j = i // N_MIXERS
        mod_ctx = (silu_ctx @ mod_w[i] + mod_b[i])[None, None, :]
        mod_lat = (silu_c @ mod_w[i] + mod_b[i])[:, None, :]
        s1c, sc1c, g1c, s2c, sc2c, g2c = jnp.split(mod_ctx, 6, axis=-1)
        s1l, sc1l, g1l, s2l, sc2l, g2l = jnp.split(mod_lat, 6, axis=-1)

        u_ctx = modulate_norm(h_ctx, norm_mix_pre[i], s1c, sc1c)
        u_lat = modulate_norm(h_lat, norm_mix_pre[i], s1l, sc1l)
        if i % N_MIXERS == 0:
            y_ctx = short_conv_mixer(u_ctx, conv_w_in[j], conv_k[j], conv_w_out[j])
            y_lat = short_conv_mixer(u_lat, conv_w_in[j], conv_k[j], conv_w_out[j])
        else:
            q_c, k_c, v_c = qkv_heads(u_ctx, attn_w_qkv[j], attn_q_gain[j], attn_k_gain[j])
            y_ctx = block_attention(q_c, k_c, v_c) @ attn_w_o[j]
            new_k_list.append(k_c)
            new_v_list.append(v_c)
            q_l, k_l, v_l = qkv_heads(u_lat, attn_w_qkv[j], attn_q_gain[j], attn_k_gain[j])
            q_l = rope_2d(q_l, row, col)
            k_l = rope_2d(k_l, row, col)
            k_all = jnp.concatenate([cache_k[:, j].astype(k_l.dtype), k_l], axis=1)
            v_all = jnp.concatenate([cache_v[:, j].astype(v_l.dtype), v_l], axis=1)
            y_lat = block_attention(q_l, k_all, v_all) @ attn_w_o[j]
        h_ctx = gated_residual(h_ctx, y_ctx, norm_mix_post[i], g1c)
        h_lat = gated_residual(h_lat, y_lat, norm_mix_post[i], g1l)

        u_ctx = modulate_norm(h_ctx, norm_ffn_pre[i], s2c, sc2c)
        u_lat = modulate_norm(h_lat, norm_ffn_pre[i], s2l, sc2l)
        f_ctx = conv_ffn(u_ctx, ffn_w_up[i], ffn_conv[i], ffn_w_down[i])
        f_lat = conv_ffn(u_lat, ffn_w_up[i], ffn_conv[i], ffn_w_down[i])
        h_ctx = gated_residual(h_ctx, f_ctx, norm_ffn_post[i], g2c)
        h_lat = gated_residual(h_lat, f_lat, norm_ffn_post[i], g2l)

    new_k = jnp.stack(new_k_list, axis=1)
    new_v = jnp.stack(new_v_list, axis=1)
    return (h_ctx, h_lat, new_k, new_v)
```

```python
import functools

import jax
import jax.numpy as jnp
from jax import lax
from jax.experimental import pallas as pl
from jax.experimental.pallas import tpu as pltpu

EPS = 1e-6
ROPE_THETA = 10000.0
GRID_W = 64
HEAD_DIM = 128
N_KV_HEADS = 2

V7X_LANES = 128
V7X_SUBLANES = 8
V7X_BF16_ROWS = 16
V7X_VMEM_LIMIT_BYTES = 56 * 1024 * 1024

HALO = V7X_SUBLANES
CONV_CHUNK = 256

BF16 = jnp.bfloat16
F32 = jnp.float32


def _resident(block_shape, index_map):
    return pl.BlockSpec(block_shape, index_map, pipeline_mode=pl.Buffered(1))


def _rms(x):
    return x * lax.rsqrt(jnp.mean(x * x, axis=-1, keepdims=True) + EPS)


def _dot(a, b):
    return jnp.dot(a, b, preferred_element_type=F32)


def _dot_nt(a, b):
    return lax.dot_general(a, b, (((1,), (1,)), ((), ())), preferred_element_type=F32)


def _mod_kernel(c_ref, w_ref, b_ref, o_ref):
    c = c_ref[...]
    s = c * (1.0 / (1.0 + jnp.exp(-c)))
    o_ref[0] = _dot(s.astype(BF16), w_ref[0].astype(BF16)) + b_ref[0]


def _modulation(cvec, mod_w, mod_b):
    depth, d, n = mod_w.shape
    rows = cvec.shape[0]
    tn = 1536
    return pl.pallas_call(
        _mod_kernel,
        grid=(depth, n // tn),
        in_specs=[
            pl.BlockSpec((rows, d), lambda l, j: (0, 0)),
            pl.BlockSpec((1, d, tn), lambda l, j: (l, 0, j)),
            pl.BlockSpec((1, 1, tn), lambda l, j: (l, 0, j)),
        ],
        out_specs=pl.BlockSpec((1, rows, tn), lambda l, j: (l, 0, j)),
        out_shape=jax.ShapeDtypeStruct((depth, rows, n), F32),
        compiler_params=pltpu.CompilerParams(dimension_semantics=("arbitrary", "arbitrary")),
        name="modulation",
    )(cvec, mod_w, mod_b.reshape(depth, 1, n))


def _fill_u(refs, has_halo, mod_ref, gpre_ref, ubuf, t, d, mod_off):
    shift = mod_ref[0, :, mod_off * d:(mod_off + 1) * d]
    scale = mod_ref[0, :, (mod_off + 1) * d:(mod_off + 2) * d]
    gs = gpre_ref[...] * (1.0 + scale)
    if has_halo:
        xp_ref, x_ref, xn_ref = refs
        halo = jnp.concatenate([xp_ref[...], xn_ref[...]], axis=0)
        ubuf[t:t + 2 * HALO, :] = (_rms(halo) * gs + shift).astype(BF16)
    else:
        (x_ref,) = refs
    ubuf[0:t, :] = (_rms(x_ref[...]) * gs + shift).astype(BF16)
    return x_ref


def _edge_flags(tiles_per_seq):
    i = pl.program_id(0)
    pos = lax.rem(i, tiles_per_seq)
    return (pos != 0).astype(F32), (pos != tiles_per_seq - 1).astype(F32)


def _conv3(h, k, t, has_halo, flags, first_row, last_row):
    zm = h[0:t]
    if has_halo:
        zp = h[t + HALO - 1:t + HALO] * flags[0]
        zn = h[t + HALO:t + HALO + 1] * flags[1]
    else:
        zp = jnp.zeros((1, h.shape[1]), F32)
        zn = zp
    z_prev = jnp.where(first_row, zp, pltpu.roll(zm, 1, 0))
    z_next = jnp.where(last_row, zn, pltpu.roll(zm, t - 1, 0))
    return z_prev * k[0:1] + zm * k[1:2] + z_next * k[2:3]


def _finish(x_ref, y, mod_ref, gpost_ref, o_ref, d, gate_off):
    gate = mod_ref[0, :, gate_off * d:(gate_off + 1) * d]
    o_ref[...] = x_ref[...] + (gate * gpost_ref[...]) * _rms(y)


def _mixer_kernel(*refs, t, d, has_halo, tiles_per_seq):
    n_x = 3 if has_halo else 1
    x_refs = refs[:n_x]
    mod_ref, gpre_ref, gpost_ref, win_ref, ck_ref, wout_ref, o_ref, ubuf, ybuf = refs[n_x:]
    x_ref = _fill_u(x_refs, has_halo, mod_ref, gpre_ref, ubuf, t, d, 0)
    flags = _edge_flags(tiles_per_seq) if has_halo else None
    row = lax.broadcasted_iota(jnp.int32, (t, CONV_CHUNK), 0)
    first_row, last_row = row == 0, row == t - 1
    for j in range(d // CONV_CHUNK):
        c0 = j * CONV_CHUNK
        b_gate = _dot(ubuf[0:t, :], win_ref[:, c0:c0 + CONV_CHUNK])
        c_gate = _dot(ubuf[...], win_ref[:, d + c0:d + c0 + CONV_CHUNK])
        xp = _dot(ubuf[...], win_ref[:, 2 * d + c0:2 * d + c0 + CONV_CHUNK])
        conv = _conv3(c_gate * xp, ck_ref[:, c0:c0 + CONV_CHUNK], t, has_halo, flags, first_row, last_row)
        ybuf[:, c0:c0 + CONV_CHUNK] = (b_gate * conv).astype(BF16)
    y = _dot(ybuf[...], wout_ref[...])
    _finish(x_ref, y, mod_ref, gpost_ref, o_ref, d, 2)


def _ffn_kernel(*refs, t, d, d_ff, has_halo, tiles_per_seq):
    n_x = 3 if has_halo else 1
    x_refs = refs[:n_x]
    mod_ref, gpre_ref, gpost_ref, wup_ref, ck_ref, wdown_ref, o_ref, ubuf, abuf = refs[n_x:]
    x_ref = _fill_u(x_refs, has_halo, mod_ref, gpre_ref, ubuf, t, d, 3)
    flags = _edge_flags(tiles_per_seq) if has_halo else None
    row = lax.broadcasted_iota(jnp.int32, (t, CONV_CHUNK), 0)
    first_row, last_row = row == 0, row == t - 1
    for j in range(d_ff // CONV_CHUNK):
        c0 = j * CONV_CHUNK
        hg = _dot(ubuf[...], wup_ref[:, c0:c0 + CONV_CHUNK])
        hu = _dot(ubuf[...], wup_ref[:, d_ff + c0:d_ff + c0 + CONV_CHUNK])
        g = _conv3(hg, ck_ref[:, c0:c0 + CONV_CHUNK], t, has_halo, flags, first_row, last_row)
        u = _conv3(hu, ck_ref[:, d_ff + c0:d_ff + c0 + CONV_CHUNK], t, has_halo, flags, first_row, last_row)
        act = g * (1.0 / (1.0 + jnp.exp(-g))) * u
        abuf[:, c0:c0 + CONV_CHUNK] = act.astype(BF16)
    y = _dot(abuf[...], wdown_ref[...])
    _finish(x_ref, y, mod_ref, gpost_ref, o_ref, d, 5)


def _conv_sublayer(kernel_fn, name, x, mod, gpre, gpost, w_a, ck, w_b, seq_len, t, inner):
    rows, d = x.shape
    assert rows % t == 0 and seq_len % t == 0 and t % V7X_BF16_ROWS == 0
    tiles_per_seq = seq_len // t
    has_halo = tiles_per_seq > 1
    per_seq_mod = mod.shape[0] > 1
    hb = t // HALO
    last_hb = rows // HALO - 1

    def mod_map(i):
        return ((i // tiles_per_seq) if per_seq_mod else 0, 0, 0)

    x_specs = [pl.BlockSpec((t, d), lambda i: (i, 0))]
    x_args = [x]
    if has_halo:
        x_specs = [pl.BlockSpec((HALO, d), lambda i: (jnp.maximum(i * hb - 1, 0), 0)),
                   x_specs[0],
                   pl.BlockSpec((HALO, d), lambda i: (jnp.minimum((i + 1) * hb, last_hb), 0))]
        x_args = [x, x, x]
    m_rows = t + 2 * HALO if has_halo else t
    body = functools.partial(kernel_fn, t=t, d=d, has_halo=has_halo, tiles_per_seq=tiles_per_seq)
    return pl.pallas_call(
        body,
        grid=(rows // t,),
        in_specs=x_specs + [
            pl.BlockSpec((1, 1, mod.shape[2]), mod_map),
            _resident((1, d), lambda i: (0, 0)),
            _resident((1, d), lambda i: (0, 0)),
            _resident(w_a.shape, lambda i: (0, 0)),
            _resident(ck.shape, lambda i: (0, 0)),
            _resident(w_b.shape, lambda i: (0, 0)),
        ],
        out_specs=pl.BlockSpec((t, d), lambda i: (i, 0)),
        out_shape=jax.ShapeDtypeStruct((rows, d), F32),
        scratch_shapes=[pltpu.VMEM((m_rows, d), BF16), pltpu.VMEM((t, inner), BF16)],
        compiler_params=pltpu.CompilerParams(dimension_semantics=("arbitrary",),
                                             vmem_limit_bytes=V7X_VMEM_LIMIT_BYTES),
        name=name,
    )(*x_args, mod, gpre, gpost, w_a, ck, w_b)


def _mixer(name, x, mod, gpre, gpost, w_in, ck, w_out, seq_len, t):
    return _conv_sublayer(_mixer_kernel, name, x, mod, gpre, gpost, w_in, ck, w_out, seq_len, t, x.shape[1])


def _ffn(name, x, mod, gpre, gpost, w_up, ck, w_down, seq_len, t):
    d_ff = w_down.shape[0]
    return _conv_sublayer(functools.partial(_ffn_kernel, d_ff=d_ff), name, x, mod, gpre, gpost,
                          w_up, ck, w_down, seq_len, t, d_ff)


def _qkv_kernel(*refs, d, n_heads, rope, emit_f32_kv):
    x_ref, mod_ref, gpre_ref, w_ref, qg_ref, kg_ref = refs[:6]
    refs = refs[6:]
    if rope:
        cos_ref, sina_ref, sinb_ref = refs[:3]
        refs = refs[3:]
    q_ref, k_ref, v_ref = refs[:3]
    kv32_refs = refs[3:]
    shift = mod_ref[0, :, 0:d]
    scale = mod_ref[0, :, d:2 * d]
    u = (_rms(x_ref[...]) * (gpre_ref[...] * (1.0 + scale)) + shift).astype(BF16)
    qkv = _dot(u, w_ref[...])
    q_scale = HEAD_DIM ** -0.5

    def head(idx, gain):
        hv = _rms(qkv[:, idx * HEAD_DIM:(idx + 1) * HEAD_DIM]) * gain
        if rope:
            hv = (hv * cos_ref[...]
                  + pltpu.roll(hv, HEAD_DIM - HEAD_DIM // 4, 1) * sina_ref[...]
                  + pltpu.roll(hv, HEAD_DIM // 4, 1) * sinb_ref[...])
        return hv

    for hq in range(n_heads):
        q_ref[:, hq * HEAD_DIM:(hq + 1) * HEAD_DIM] = (head(hq, qg_ref[...]) * q_scale).astype(BF16)
    for hk in range(N_KV_HEADS):
        kh = head(n_heads + hk, kg_ref[...])
        k_ref[:, hk * HEAD_DIM:(hk + 1) * HEAD_DIM] = kh.astype(BF16)
        if emit_f32_kv:
            kv32_refs[0][:, hk * HEAD_DIM:(hk + 1) * HEAD_DIM] = kh
    v0 = (n_heads + N_KV_HEADS) * HEAD_DIM
    v = qkv[:, v0:v0 + N_KV_HEADS * HEAD_DIM]
    v_ref[...] = v.astype(BF16)
    if emit_f32_kv:
        kv32_refs[1][...] = v


def _qkv(name, x, mod, gpre, w_qkv, q_gain, k_gain, seq_len, t, rope_tables=None, emit_f32_kv=False):
    rows, d = x.shape
    n_heads = d // HEAD_DIM
    kv_w = N_KV_HEADS * HEAD_DIM
    tiles_per_seq = seq_len // t
    per_seq_mod = mod.shape[0] > 1
    rope = rope_tables is not None

    def mod_map(i):
        return ((i // tiles_per_seq) if per_seq_mod else 0, 0, 0)

    in_specs = [
        pl.BlockSpec((t, d), lambda i: (i, 0)),
        pl.BlockSpec((1, 1, mod.shape[2]), mod_map),
        _resident((1, d), lambda i: (0, 0)),
        _resident(w_qkv.shape, lambda i: (0, 0)),
        _resident((1, HEAD_DIM), lambda i: (0, 0)),
        _resident((1, HEAD_DIM), lambda i: (0, 0)),
    ]
    args = [x, mod, gpre, w_qkv, q_gain, k_gain]
    if rope:
        in_specs += [pl.BlockSpec((t, HEAD_DIM), lambda i: (i % tiles_per_seq, 0))] * 3
        args += list(rope_tables)
    out_specs = [pl.BlockSpec((t, d), lambda i: (i, 0)),
                 pl.BlockSpec((t, kv_w), lambda i: (i, 0)),
                 pl.BlockSpec((t, kv_w), lambda i: (i, 0))]
    out_shape = [jax.ShapeDtypeStruct((rows, d), BF16),
                 jax.ShapeDtypeStruct((rows, kv_w), BF16),
                 jax.ShapeDtypeStruct((rows, kv_w), BF16)]
    if emit_f32_kv:
        out_specs += [pl.BlockSpec((t, kv_w), lambda i: (i, 0))] * 2
        out_shape += [jax.ShapeDtypeStruct((rows, kv_w), F32)] * 2
    return pl.pallas_call(
        functools.partial(_qkv_kernel, d=d, n_heads=n_heads, rope=rope, emit_f32_kv=emit_f32_kv),
        grid=(rows // t,),
        in_specs=in_specs,
        out_specs=out_specs,
        out_shape=out_shape,
        compiler_params=pltpu.CompilerParams(dimension_semantics=("arbitrary",),
                                             vmem_limit_bytes=V7X_VMEM_LIMIT_BYTES),
        name=name,
    )(*args)


def _attn_kernel(*refs, d, n_heads, n_kv_sets):
    x_ref, q_ref = refs[:2]
    kv_refs = refs[2:2 + 2 * n_kv_sets]
    mod_ref, gpost_ref, wo_ref, o_ref, obuf = refs[2 + 2 * n_kv_sets:]
    tq = q_ref.shape[0]
    group = n_heads // N_KV_HEADS
    for kk in range(N_KV_HEADS):
        lanes = slice(kk * HEAD_DIM, (kk + 1) * HEAD_DIM)
        qs = jnp.concatenate(
            [q_ref[:, (kk * group + g) * HEAD_DIM:(kk * group + g + 1) * HEAD_DIM] for g in range(group)], axis=0)
        scores = [_dot_nt(qs, kv_refs[2 * s][:, lanes]) for s in range(n_kv_sets)]
        m = scores[0].max(axis=-1, keepdims=True)
        for sc in scores[1:]:
            m = jnp.maximum(m, sc.max(axis=-1, keepdims=True))
        denom = None
        acc = None
        for s, sc in enumerate(scores):
            p = jnp.exp(sc - m)
            ps = p.sum(axis=-1, keepdims=True)
            pv = _dot(p.astype(BF16), kv_refs[2 * s + 1][:, lanes])
            denom = ps if denom is None else denom + ps
            acc = pv if acc is None else acc + pv
        out = acc * (1.0 / denom)
        for g in range(group):
            hq = kk * group + g
            obuf[:, hq * HEAD_DIM:(hq + 1) * HEAD_DIM] = out[g * tq:(g + 1) * tq].astype(BF16)
    y = _dot(obuf[...], wo_ref[...])
    _finish(x_ref, y, mod_ref, gpost_ref, o_ref, d, 2)


def _attention(name, x, q, kv_sets, mod, gpost, w_o, seq_len, tq):
    rows, d = x.shape
    n_heads = d // HEAD_DIM
    kv_w = N_KV_HEADS * HEAD_DIM
    tiles_per_seq = seq_len // tq
    per_seq_mod = mod.shape[0] > 1

    def mod_map(i):
        return ((i // tiles_per_seq) if per_seq_mod else 0, 0, 0)

    in_specs = [pl.BlockSpec((tq, d), lambda i: (i, 0)), pl.BlockSpec((tq, d), lambda i: (i, 0))]
    args = [x, q]
    for k, v, n in kv_sets:
        in_specs += [pl.BlockSpec((n, kv_w), lambda i: (i // tiles_per_seq, 0))] * 2
        args += [k, v]
    in_specs += [pl.BlockSpec((1, 1, mod.shape[2]), mod_map),
                 _resident((1, d), lambda i: (0, 0)),
                 _resident(w_o.shape, lambda i: (0, 0))]
    args += [mod, gpost, w_o]
    return pl.pallas_call(
        functools.partial(_attn_kernel, d=d, n_heads=n_heads, n_kv_sets=len(kv_sets)),
        grid=(rows // tq,),
        in_specs=in_specs,
        out_specs=pl.BlockSpec((tq, d), lambda i: (i, 0)),
        out_shape=jax.ShapeDtypeStruct((rows, d), F32),
        scratch_shapes=[pltpu.VMEM((tq, d), BF16)],
        compiler_params=pltpu.CompilerParams(dimension_semantics=("arbitrary",),
                                             vmem_limit_bytes=V7X_VMEM_LIMIT_BYTES),
        name=name,
    )(*args)


def _rope_tables(n_lat):
    half = HEAD_DIM // 2
    rows_n = n_lat // GRID_W
    row = jnp.repeat(jnp.arange(rows_n, dtype=jnp.int32), GRID_W)
    col = jnp.tile(jnp.arange(GRID_W, dtype=jnp.int32), rows_n)
    inv = ROPE_THETA ** (-jnp.arange(0, half, 2, dtype=F32) / half)
    ang_r = row.astype(F32)[:, None] * inv[None, :]
    ang_c = col.astype(F32)[:, None] * inv[None, :]
    cr, sr, cc, sc = jnp.cos(ang_r), jnp.sin(ang_r), jnp.cos(ang_c), jnp.sin(ang_c)
    zero = jnp.zeros_like(sr)
    cos = jnp.concatenate([cr, cr, cc, cc], axis=-1)
    sin_a = jnp.concatenate([-sr, zero, -sc, zero], axis=-1)
    sin_b = jnp.concatenate([zero, sr, zero, sc], axis=-1)
    return cos, sin_a, sin_b


def kernel(x_prompt, x_sample, cache_k, cache_v, c, c_ctx, mod_w, mod_b, norm_mix_pre, norm_mix_post, norm_ffn_pre, norm_ffn_post, conv_w_in, conv_k, conv_w_out, attn_w_qkv, attn_q_gain, attn_k_gain, attn_w_o, ffn_w_up, ffn_conv, ffn_w_down):
    batch, seq, d = x_prompt.shape
    dec_batch, dec_seq, _ = x_sample.shape
    depth = mod_w.shape[0]
    past_len = cache_k.shape[2]
    kv_w = N_KV_HEADS * HEAD_DIM
    t_ctx = seq
    t_lat = 512
    tq_lat = 128

    n_cond = dec_batch + 1
    pad = (-n_cond) % V7X_BF16_ROWS
    cvec = jnp.concatenate([c, c_ctx[None, :], jnp.zeros((pad, d), F32)], axis=0)
    mod_all = _modulation(cvec, mod_w, mod_b)

    h_ctx = x_prompt.reshape(batch * seq, d)
    h_lat = x_sample.reshape(dec_batch * dec_seq, d)
    rope_tables = _rope_tables(dec_seq)
    new_k, new_v = [], []
    row2 = lambda a: a.reshape(1, -1)

    for i in range(depth):
        j = i // 2
        mod_lat = mod_all[i, :dec_batch][:, None, :]
        mod_ctx = mod_all[i, dec_batch:dec_batch + 1][:, None, :]
        g_mix_pre, g_mix_post = row2(norm_mix_pre[i]), row2(norm_mix_post[i])
        g_ffn_pre, g_ffn_post = row2(norm_ffn_pre[i]), row2(norm_ffn_post[i])
        if i % 2 == 0:
            w_in, w_out = conv_w_in[j].astype(BF16), conv_w_out[j].astype(BF16)
            h_ctx = _mixer(f"mixer{i}_ctx", h_ctx, mod_ctx, g_mix_pre, g_mix_post, w_in, conv_k[j], w_out, seq, t_ctx)
            h_lat = _mixer(f"mixer{i}_lat", h_lat, mod_lat, g_mix_pre, g_mix_post, w_in, conv_k[j], w_out, dec_seq, t_lat)
        else:
            w_qkv, w_o = attn_w_qkv[j].astype(BF16), attn_w_o[j].astype(BF16)
            qg, kg = row2(attn_q_gain[j]), row2(attn_k_gain[j])
            q_c, k_c, v_c, k_c32, v_c32 = _qkv(f"qkv{i}_ctx", h_ctx, mod_ctx, g_mix_pre, w_qkv, qg, kg, seq, t_ctx,
                                               emit_f32_kv=True)
            new_k.append(k_c32.reshape(batch, seq, N_KV_HEADS, HEAD_DIM))
            new_v.append(v_c32.reshape(batch, seq, N_KV_HEADS, HEAD_DIM))
            h_ctx = _attention(f"attn{i}_ctx", h_ctx, q_c, [(k_c, v_c, seq)], mod_ctx, g_mix_post, w_o, seq, t_ctx)
            q_l, k_l, v_l = _qkv(f"qkv{i}_lat", h_lat, mod_lat, g_mix_pre, w_qkv, qg, kg, dec_seq, t_lat,
                                 rope_tables=rope_tables)
            ck = cache_k[:, j].reshape(dec_batch * past_len, kv_w).astype(BF16)
            cv = cache_v[:, j].reshape(dec_batch * past_len, kv_w).astype(BF16)
            h_lat = _attention(f"attn{i}_lat", h_lat, q_l, [(ck, cv, past_len), (k_l, v_l, dec_seq)],
                               mod_lat, g_mix_post, w_o, dec_seq, tq_lat)
        w_up, w_down = ffn_w_up[i].astype(BF16), ffn_w_down[i].astype(BF16)
        h_ctx = _ffn(f"ffn{i}_ctx", h_ctx, mod_ctx, g_ffn_pre, g_ffn_post, w_up, ffn_conv[i], w_down, seq, t_ctx)
        h_lat = _ffn(f"ffn{i}_lat", h_lat, mod_lat, g_ffn_pre, g_ffn_post, w_up, ffn_conv[i], w_down, dec_seq, t_lat)

    return (h_ctx.reshape(batch, seq, d), h_lat.reshape(dec_batch, dec_seq, d),
            jnp.stack(new_k, axis=1), jnp.stack(new_v, axis=1))
```

```python
import functools

import jax
import jax.numpy as jnp
from jax import lax
from jax.experimental import pallas as pl
from jax.experimental.pallas import tpu as pltpu

EPS = 1e-6
ROPE_THETA = 10000.0
GRID_W = 64
HEAD_DIM = 128
N_KV_HEADS = 2

V7X_LANES = 128
V7X_SUBLANES = 8
V7X_BF16_ROWS = 16
V7X_VMEM_LIMIT_BYTES = 56 * 1024 * 1024

HALO = V7X_SUBLANES
CONV_CHUNK = 256
KEY_BLOCK = 1024
LOG2_E = 1.4426950408889634

BF16 = jnp.bfloat16
F32 = jnp.float32


def _resident(block_shape, index_map):
    return pl.BlockSpec(block_shape, index_map, pipeline_mode=pl.Buffered(1))


def _rms(x):
    return x * lax.rsqrt(jnp.mean(x * x, axis=-1, keepdims=True) + EPS)


def _dot(a, b):
    return jnp.dot(a, b, preferred_element_type=F32)


def _dot_nt(a, b):
    return lax.dot_general(a, b, (((1,), (1,)), ((), ())), preferred_element_type=F32)


def _mod_kernel(c_ref, w_ref, b_ref, o_ref):
    c = c_ref[...]
    s = c * (1.0 / (1.0 + jnp.exp(-c)))
    o_ref[0] = _dot(s.astype(BF16), w_ref[0].astype(BF16)) + b_ref[0]


def _modulation(cvec, mod_w, mod_b):
    depth, d, n = mod_w.shape
    rows = cvec.shape[0]
    tn = 1536
    return pl.pallas_call(
        _mod_kernel,
        grid=(depth, n // tn),
        in_specs=[
            pl.BlockSpec((rows, d), lambda l, j: (0, 0)),
            pl.BlockSpec((1, d, tn), lambda l, j: (l, 0, j)),
            pl.BlockSpec((1, 1, tn), lambda l, j: (l, 0, j)),
        ],
        out_specs=pl.BlockSpec((1, rows, tn), lambda l, j: (l, 0, j)),
        out_shape=jax.ShapeDtypeStruct((depth, rows, n), F32),
        compiler_params=pltpu.CompilerParams(dimension_semantics=("arbitrary", "arbitrary")),
        name="modulation",
    )(cvec, mod_w, mod_b.reshape(depth, 1, n))


def _fill_u(refs, has_halo, mod_ref, gpre_ref, ubuf, t, d, mod_off):
    shift = mod_ref[0, :, mod_off * d:(mod_off + 1) * d]
    scale = mod_ref[0, :, (mod_off + 1) * d:(mod_off + 2) * d]
    gs = gpre_ref[...] * (1.0 + scale)
    if has_halo:
        xp_ref, x_ref, xn_ref = refs
        halo = jnp.concatenate([xp_ref[...], xn_ref[...]], axis=0)
        ubuf[t:t + 2 * HALO, :] = (_rms(halo) * gs + shift).astype(BF16)
    else:
        (x_ref,) = refs
    ubuf[0:t, :] = (_rms(x_ref[...]) * gs + shift).astype(BF16)
    return x_ref


def _edge_flags(tiles_per_seq):
    i = pl.program_id(0)
    pos = lax.rem(i, tiles_per_seq)
    return (pos != 0).astype(F32), (pos != tiles_per_seq - 1).astype(F32)


def _conv3(h, k, t, has_halo, flags, first_row, last_row):
    zm = h[0:t]
    if has_halo:
        zp = h[t + HALO - 1:t + HALO] * flags[0]
        zn = h[t + HALO:t + HALO + 1] * flags[1]
    else:
        zp = jnp.zeros((1, h.shape[1]), F32)
        zn = zp
    z_prev = jnp.where(first_row, zp, pltpu.roll(zm, 1, 0))
    z_next = jnp.where(last_row, zn, pltpu.roll(zm, t - 1, 0))
    return z_prev * k[0:1] + zm * k[1:2] + z_next * k[2:3]


def _finish(x_ref, y, mod_ref, gpost_ref, o_ref, d, gate_off):
    gate = mod_ref[0, :, gate_off * d:(gate_off + 1) * d]
    o_ref[...] = x_ref[...] + (gate * gpost_ref[...]) * _rms(y)


def _mixer_kernel(*refs, t, d, has_halo, tiles_per_seq):
    n_x = 3 if has_halo else 1
    x_refs = refs[:n_x]
    mod_ref, gpre_ref, gpost_ref, win_ref, ck_ref, wout_ref, o_ref, ubuf, ybuf = refs[n_x:]
    x_ref = _fill_u(x_refs, has_halo, mod_ref, gpre_ref, ubuf, t, d, 0)
    flags = _edge_flags(tiles_per_seq) if has_halo else None
    row = lax.broadcasted_iota(jnp.int32, (t, CONV_CHUNK), 0)
    first_row, last_row = row == 0, row == t - 1
    for j in range(d // CONV_CHUNK):
        c0 = j * CONV_CHUNK
        b_gate = _dot(ubuf[0:t, :], win_ref[:, c0:c0 + CONV_CHUNK])
        c_gate = _dot(ubuf[...], win_ref[:, d + c0:d + c0 + CONV_CHUNK])
        xp = _dot(ubuf[...], win_ref[:, 2 * d + c0:2 * d + c0 + CONV_CHUNK])
        conv = _conv3(c_gate * xp, ck_ref[:, c0:c0 + CONV_CHUNK], t, has_halo, flags, first_row, last_row)
        ybuf[:, c0:c0 + CONV_CHUNK] = (b_gate * conv).astype(BF16)
    y = _dot(ybuf[...], wout_ref[...])
    _finish(x_ref, y, mod_ref, gpost_ref, o_ref, d, 2)


def _ffn_kernel(*refs, t, d, d_ff, has_halo, tiles_per_seq):
    n_x = 3 if has_halo else 1
    x_refs = refs[:n_x]
    mod_ref, gpre_ref, gpost_ref, wup_ref, ck_ref, wdown_ref, o_ref, ubuf, abuf = refs[n_x:]
    x_ref = _fill_u(x_refs, has_halo, mod_ref, gpre_ref, ubuf, t, d, 3)
    flags = _edge_flags(tiles_per_seq) if has_halo else None
    row = lax.broadcasted_iota(jnp.int32, (t, CONV_CHUNK), 0)
    first_row, last_row = row == 0, row == t - 1
    for j in range(d_ff // CONV_CHUNK):
        c0 = j * CONV_CHUNK
        hg = _dot(ubuf[...], wup_ref[:, c0:c0 + CONV_CHUNK])
        hu = _dot(ubuf[...], wup_ref[:, d_ff + c0:d_ff + c0 + CONV_CHUNK])
        g = _conv3(hg, ck_ref[:, c0:c0 + CONV_CHUNK], t, has_halo, flags, first_row, last_row)
        u = _conv3(hu, ck_ref[:, d_ff + c0:d_ff + c0 + CONV_CHUNK], t, has_halo, flags, first_row, last_row)
        act = g * (1.0 / (1.0 + jnp.exp(-g))) * u
        abuf[:, c0:c0 + CONV_CHUNK] = act.astype(BF16)
    y = _dot(abuf[...], wdown_ref[...])
    _finish(x_ref, y, mod_ref, gpost_ref, o_ref, d, 5)


def _conv_sublayer(kernel_fn, name, x, mod, gpre, gpost, w_a, ck, w_b, seq_len, t, inner):
    rows, d = x.shape
    assert rows % t == 0 and seq_len % t == 0 and t % V7X_BF16_ROWS == 0
    tiles_per_seq = seq_len // t
    has_halo = tiles_per_seq > 1
    per_seq_mod = mod.shape[0] > 1
    hb = t // HALO
    last_hb = rows // HALO - 1

    def mod_map(i):
        return ((i // tiles_per_seq) if per_seq_mod else 0, 0, 0)

    x_specs = [pl.BlockSpec((t, d), lambda i: (i, 0))]
    x_args = [x]
    if has_halo:
        x_specs = [pl.BlockSpec((HALO, d), lambda i: (jnp.maximum(i * hb - 1, 0), 0)),
                   x_specs[0],
                   pl.BlockSpec((HALO, d), lambda i: (jnp.minimum((i + 1) * hb, last_hb), 0))]
        x_args = [x, x, x]
    m_rows = t + 2 * HALO if has_halo else t
    body = functools.partial(kernel_fn, t=t, d=d, has_halo=has_halo, tiles_per_seq=tiles_per_seq)
    return pl.pallas_call(
        body,
        grid=(rows // t,),
        in_specs=x_specs + [
            pl.BlockSpec((1, 1, mod.shape[2]), mod_map),
            _resident((1, d), lambda i: (0, 0)),
            _resident((1, d), lambda i: (0, 0)),
            _resident(w_a.shape, lambda i: (0, 0)),
            _resident(ck.shape, lambda i: (0, 0)),
            _resident(w_b.shape, lambda i: (0, 0)),
        ],
        out_specs=pl.BlockSpec((t, d), lambda i: (i, 0)),
        out_shape=jax.ShapeDtypeStruct((rows, d), F32),
        scratch_shapes=[pltpu.VMEM((m_rows, d), BF16), pltpu.VMEM((t, inner), BF16)],
        compiler_params=pltpu.CompilerParams(dimension_semantics=("arbitrary",),
                                             vmem_limit_bytes=V7X_VMEM_LIMIT_BYTES),
        name=name,
    )(*x_args, mod, gpre, gpost, w_a, ck, w_b)


def _mixer(name, x, mod, gpre, gpost, w_in, ck, w_out, seq_len, t):
    return _conv_sublayer(_mixer_kernel, name, x, mod, gpre, gpost, w_in, ck, w_out, seq_len, t, x.shape[1])


def _ffn(name, x, mod, gpre, gpost, w_up, ck, w_down, seq_len, t):
    d_ff = w_down.shape[0]
    return _conv_sublayer(functools.partial(_ffn_kernel, d_ff=d_ff), name, x, mod, gpre, gpost,
                          w_up, ck, w_down, seq_len, t, d_ff)


def _qkv_kernel(*refs, d, n_heads, rope, emit_f32_kv):
    x_ref, mod_ref, gpre_ref, w_ref, qg_ref, kg_ref = refs[:6]
    refs = refs[6:]
    if rope:
        cos_ref, sina_ref, sinb_ref = refs[:3]
        refs = refs[3:]
    q_ref, k_ref, v_ref = refs[:3]
    kv32_refs = refs[3:]
    shift = mod_ref[0, :, 0:d]
    scale = mod_ref[0, :, d:2 * d]
    u = (_rms(x_ref[...]) * (gpre_ref[...] * (1.0 + scale)) + shift).astype(BF16)
    qkv = _dot(u, w_ref[...])
    q_scale = HEAD_DIM ** -0.5 * LOG2_E

    def head(idx, gain):
        hv = _rms(qkv[:, idx * HEAD_DIM:(idx + 1) * HEAD_DIM]) * gain
        if rope:
            hv = (hv * cos_ref[...]
                  + pltpu.roll(hv, HEAD_DIM - HEAD_DIM // 4, 1) * sina_ref[...]
                  + pltpu.roll(hv, HEAD_DIM // 4, 1) * sinb_ref[...])
        return hv

    for hq in range(n_heads):
        q_ref[:, hq * HEAD_DIM:(hq + 1) * HEAD_DIM] = (head(hq, qg_ref[...]) * q_scale).astype(BF16)
    for hk in range(N_KV_HEADS):
        kh = head(n_heads + hk, kg_ref[...])
        k_ref[:, hk * HEAD_DIM:(hk + 1) * HEAD_DIM] = kh.astype(BF16)
        if emit_f32_kv:
            kv32_refs[0][:, hk * HEAD_DIM:(hk + 1) * HEAD_DIM] = kh
    v0 = (n_heads + N_KV_HEADS) * HEAD_DIM
    v = qkv[:, v0:v0 + N_KV_HEADS * HEAD_DIM]
    v_ref[...] = v.astype(BF16)
    if emit_f32_kv:
        kv32_refs[1][...] = v


def _qkv(name, x, mod, gpre, w_qkv, q_gain, k_gain, seq_len, t, rope_tables=None, emit_f32_kv=False):
    rows, d = x.shape
    n_heads = d // HEAD_DIM
    kv_w = N_KV_HEADS * HEAD_DIM
    tiles_per_seq = seq_len // t
    per_seq_mod = mod.shape[0] > 1
    rope = rope_tables is not None

    def mod_map(i):
        return ((i // tiles_per_seq) if per_seq_mod else 0, 0, 0)

    in_specs = [
        pl.BlockSpec((t, d), lambda i: (i, 0)),
        pl.BlockSpec((1, 1, mod.shape[2]), mod_map),
        _resident((1, d), lambda i: (0, 0)),
        _resident(w_qkv.shape, lambda i: (0, 0)),
        _resident((1, HEAD_DIM), lambda i: (0, 0)),
        _resident((1, HEAD_DIM), lambda i: (0, 0)),
    ]
    args = [x, mod, gpre, w_qkv, q_gain, k_gain]
    if rope:
        in_specs += [pl.BlockSpec((t, HEAD_DIM), lambda i: (i % tiles_per_seq, 0))] * 3
        args += list(rope_tables)
    out_specs = [pl.BlockSpec((t, d), lambda i: (i, 0)),
                 pl.BlockSpec((t, kv_w), lambda i: (i, 0)),
                 pl.BlockSpec((t, kv_w), lambda i: (i, 0))]
    out_shape = [jax.ShapeDtypeStruct((rows, d), BF16),
                 jax.ShapeDtypeStruct((rows, kv_w), BF16),
                 jax.ShapeDtypeStruct((rows, kv_w), BF16)]
    if emit_f32_kv:
        out_specs += [pl.BlockSpec((t, kv_w), lambda i: (i, 0))] * 2
        out_shape += [jax.ShapeDtypeStruct((rows, kv_w), F32)] * 2
    return pl.pallas_call(
        functools.partial(_qkv_kernel, d=d, n_heads=n_heads, rope=rope, emit_f32_kv=emit_f32_kv),
        grid=(rows // t,),
        in_specs=in_specs,
        out_specs=out_specs,
        out_shape=out_shape,
        compiler_params=pltpu.CompilerParams(dimension_semantics=("arbitrary",),
                                             vmem_limit_bytes=V7X_VMEM_LIMIT_BYTES),
        name=name,
    )(*args)


def _attn_kernel(*refs, d, n_heads, n_kv_sets):
    x_ref, q_ref = refs[:2]
    kv_refs = refs[2:2 + 2 * n_kv_sets]
    mod_ref, gpost_ref, wo_ref, o_ref, obuf = refs[2 + 2 * n_kv_sets:]
    tq = q_ref.shape[0]
    group = n_heads // N_KV_HEADS
    for kk in range(N_KV_HEADS):
        lanes = slice(kk * HEAD_DIM, (kk + 1) * HEAD_DIM)
        qs = jnp.concatenate(
            [q_ref[:, (kk * group + g) * HEAD_DIM:(kk * group + g + 1) * HEAD_DIM] for g in range(group)], axis=0)
        m = denom = acc = None
        for s in range(n_kv_sets):
            k_ref, v_ref = kv_refs[2 * s], kv_refs[2 * s + 1]
            n_keys = k_ref.shape[0]
            kb = min(KEY_BLOCK, n_keys)
            for b0 in range(0, n_keys, kb):
                sc = _dot_nt(qs, k_ref[b0:b0 + kb, lanes])
                mb = sc.max(axis=-1, keepdims=True)
                if m is None:
                    m = mb
                    p = jnp.exp2(sc - m)
                    denom = p.sum(axis=-1, keepdims=True)
                    acc = _dot(p.astype(BF16), v_ref[b0:b0 + kb, lanes])
                else:
                    m_new = jnp.maximum(m, mb)
                    alpha = jnp.exp2(m - m_new)
                    p = jnp.exp2(sc - m_new)
                    denom = alpha * denom + p.sum(axis=-1, keepdims=True)
                    acc = alpha * acc + _dot(p.astype(BF16), v_ref[b0:b0 + kb, lanes])
                    m = m_new
        out = acc * (1.0 / denom)
        for g in range(group):
            hq = kk * group + g
            obuf[:, hq * HEAD_DIM:(hq + 1) * HEAD_DIM] = out[g * tq:(g + 1) * tq].astype(BF16)
    y = _dot(obuf[...], wo_ref[...])
    _finish(x_ref, y, mod_ref, gpost_ref, o_ref, d, 2)


def _attention(name, x, q, kv_sets, mod, gpost, w_o, seq_len, tq):
    rows, d = x.shape
    n_heads = d // HEAD_DIM
    kv_w = N_KV_HEADS * HEAD_DIM
    tiles_per_seq = seq_len // tq
    per_seq_mod = mod.shape[0] > 1

    def mod_map(i):
        return ((i // tiles_per_seq) if per_seq_mod else 0, 0, 0)

    in_specs = [pl.BlockSpec((tq, d), lambda i: (i, 0)), pl.BlockSpec((tq, d), lambda i: (i, 0))]
    args = [x, q]
    for k, v, n in kv_sets:
        in_specs += [pl.BlockSpec((n, kv_w), lambda i: (i // tiles_per_seq, 0))] * 2
        args += [k, v]
    in_specs += [pl.BlockSpec((1, 1, mod.shape[2]), mod_map),
                 _resident((1, d), lambda i: (0, 0)),
                 _resident(w_o.shape, lambda i: (0, 0))]
    args += [mod, gpost, w_o]
    return pl.pallas_call(
        functools.partial(_attn_kernel, d=d, n_heads=n_heads, n_kv_sets=len(kv_sets)),
        grid=(rows // tq,),
        in_specs=in_specs,
        out_specs=pl.BlockSpec((tq, d), lambda i: (i, 0)),
        out_shape=jax.ShapeDtypeStruct((rows, d), F32),
        scratch_shapes=[pltpu.VMEM((tq, d), BF16)],
        compiler_params=pltpu.CompilerParams(dimension_semantics=("arbitrary",),
                                             vmem_limit_bytes=V7X_VMEM_LIMIT_BYTES),
        name=name,
    )(*args)


def _rope_tables(n_lat):
    half = HEAD_DIM // 2
    rows_n = n_lat // GRID_W
    row = jnp.repeat(jnp.arange(rows_n, dtype=jnp.int32), GRID_W)
    col = jnp.tile(jnp.arange(GRID_W, dtype=jnp.int32), rows_n)
    inv = ROPE_THETA ** (-jnp.arange(0, half, 2, dtype=F32) / half)
    ang_r = row.astype(F32)[:, None] * inv[None, :]
    ang_c = col.astype(F32)[:, None] * inv[None, :]
    cr, sr, cc, sc = jnp.cos(ang_r), jnp.sin(ang_r), jnp.cos(ang_c), jnp.sin(ang_c)
    zero = jnp.zeros_like(sr)
    cos = jnp.concatenate([cr, cr, cc, cc], axis=-1)
    sin_a = jnp.concatenate([-sr, zero, -sc, zero], axis=-1)
    sin_b = jnp.concatenate([zero, sr, zero, sc], axis=-1)
    return cos, sin_a, sin_b


def kernel(x_prompt, x_sample, cache_k, cache_v, c, c_ctx, mod_w, mod_b, norm_mix_pre, norm_mix_post, norm_ffn_pre, norm_ffn_post, conv_w_in, conv_k, conv_w_out, attn_w_qkv, attn_q_gain, attn_k_gain, attn_w_o, ffn_w_up, ffn_conv, ffn_w_down):
    batch, seq, d = x_prompt.shape
    dec_batch, dec_seq, _ = x_sample.shape
    depth = mod_w.shape[0]
    past_len = cache_k.shape[2]
    kv_w = N_KV_HEADS * HEAD_DIM
    t_ctx = seq
    t_lat = 512
    tq_lat = 256

    n_cond = dec_batch + 1
    pad = (-n_cond) % V7X_BF16_ROWS
    cvec = jnp.concatenate([c, c_ctx[None, :], jnp.zeros((pad, d), F32)], axis=0)
    mod_all = _modulation(cvec, mod_w, mod_b)

    h_ctx = x_prompt.reshape(batch * seq, d)
    h_lat = x_sample.reshape(dec_batch * dec_seq, d)
    rope_tables = _rope_tables(dec_seq)
    new_k, new_v = [], []
    row2 = lambda a: a.reshape(1, -1)

    for i in range(depth):
        j = i // 2
        mod_lat = mod_all[i, :dec_batch][:, None, :]
        mod_ctx = mod_all[i, dec_batch:dec_batch + 1][:, None, :]
        g_mix_pre, g_mix_post = row2(norm_mix_pre[i]), row2(norm_mix_post[i])
        g_ffn_pre, g_ffn_post = row2(norm_ffn_pre[i]), row2(norm_ffn_post[i])
        if i % 2 == 0:
            w_in, w_out = conv_w_in[j].astype(BF16), conv_w_out[j].astype(BF16)
            h_ctx = _mixer(f"mixer{i}_ctx", h_ctx, mod_ctx, g_mix_pre, g_mix_post, w_in, conv_k[j], w_out, seq, t_ctx)
            h_lat = _mixer(f"mixer{i}_lat", h_lat, mod_lat, g_mix_pre, g_mix_post, w_in, conv_k[j], w_out, dec_seq, t_lat)
        else:
            w_qkv, w_o = attn_w_qkv[j].astype(BF16), attn_w_o[j].astype(BF16)
            qg, kg = row2(attn_q_gain[j]), row2(attn_k_gain[j])
            q_c, k_c, v_c, k_c32, v_c32 = _qkv(f"qkv{i}_ctx", h_ctx, mod_ctx, g_mix_pre, w_qkv, qg, kg, seq, t_ctx,
                                               emit_f32_kv=True)
            new_k.append(k_c32.reshape(batch, seq, N_KV_HEADS, HEAD_DIM))
            new_v.append(v_c32.reshape(batch, seq, N_KV_HEADS, HEAD_DIM))
            h_ctx = _attention(f"attn{i}_ctx", h_ctx, q_c, [(k_c, v_c, seq)], mod_ctx, g_mix_post, w_o, seq, t_ctx)
            q_l, k_l, v_l = _qkv(f"qkv{i}_lat", h_lat, mod_lat, g_mix_pre, w_qkv, qg, kg, dec_seq, t_lat,
                                 rope_tables=rope_tables)
            ck = cache_k[:, j].reshape(dec_batch * past_len, kv_w).astype(BF16)
            cv = cache_v[:, j].reshape(dec_batch * past_len, kv_w).astype(BF16)
            h_lat = _attention(f"attn{i}_lat", h_lat, q_l, [(ck, cv, past_len), (k_l, v_l, dec_seq)],
                               mod_lat, g_mix_post, w_o, dec_seq, tq_lat)
        w_up, w_down = ffn_w_up[i].astype(BF16), ffn_w_down[i].astype(BF16)
        h_ctx = _ffn(f"ffn{i}_ctx", h_ctx, mod_ctx, g_ffn_pre, g_ffn_post, w_up, ffn_conv[i], w_down, seq, t_ctx)
        h_lat = _ffn(f"ffn{i}_lat", h_lat, mod_lat, g_ffn_pre, g_ffn_post, w_up, ffn_conv[i], w_down, dec_seq, t_lat)

    return (h_ctx.reshape(batch, seq, d), h_lat.reshape(dec_batch, dec_seq, d),
            jnp.stack(new_k, axis=1), jnp.stack(new_v, axis=1))
```

```python
import functools

import jax
import jax.numpy as jnp
import numpy as np
from jax import lax
from jax.experimental import pallas as pl
from jax.experimental.pallas import tpu as pltpu

EPS = 1e-6
ROPE_THETA = 10000.0
GRID_W = 64
HEAD_DIM = 128
N_KV_HEADS = 2

V7X_SUBLANES = 8
V7X_BF16_ROWS = 16
V7X_VMEM_LIMIT_BYTES = 56 * 1024 * 1024

HALO = V7X_SUBLANES
CONV_CHUNK = 256
KEY_BLOCK = 1024
LOG2_E = 1.4426950408889634

BF16 = jnp.bfloat16
F32 = jnp.float32


def _resident(block_shape, index_map):
    return pl.BlockSpec(block_shape, index_map, pipeline_mode=pl.Buffered(1))


def _layer_spec(stacked, layer):
    zeros = (0,) * (stacked.ndim - 1)
    return _resident((None,) + stacked.shape[1:], lambda i: (layer,) + zeros)


def _rms(x):
    return x * lax.rsqrt(jnp.mean(x * x, axis=-1, keepdims=True) + EPS)


def _dot(a, b):
    return jnp.dot(a, b, preferred_element_type=F32)


def _dot_nt(a, b):
    return lax.dot_general(a, b, (((1,), (1,)), ((), ())), preferred_element_type=F32)


def _mod_kernel(c_ref, w_ref, b_ref, o_ref):
    c = c_ref[...]
    s = c * (1.0 / (1.0 + jnp.exp(-c)))
    o_ref[0] = _dot(s.astype(BF16), w_ref[0].astype(BF16)) + b_ref[0]


def _modulation(cvec, mod_w, mod_b):
    depth, d, n = mod_w.shape
    rows = cvec.shape[0]
    tn = 1536
    return pl.pallas_call(
        _mod_kernel,
        grid=(depth, n // tn),
        in_specs=[
            pl.BlockSpec((rows, d), lambda l, j: (0, 0)),
            pl.BlockSpec((1, d, tn), lambda l, j: (l, 0, j)),
            pl.BlockSpec((1, 1, tn), lambda l, j: (l, 0, j)),
        ],
        out_specs=pl.BlockSpec((1, rows, tn), lambda l, j: (l, 0, j)),
        out_shape=jax.ShapeDtypeStruct((depth, rows, n), F32),
        compiler_params=pltpu.CompilerParams(dimension_semantics=("arbitrary", "arbitrary")),
        name="modulation",
    )(cvec, mod_w, mod_b.reshape(depth, 1, n))


def _fill_u(refs, has_halo, mod_ref, gpre_ref, ubuf, t, d, mod_off):
    shift = mod_ref[0, :, mod_off * d:(mod_off + 1) * d]
    scale = mod_ref[0, :, (mod_off + 1) * d:(mod_off + 2) * d]
    gs = gpre_ref[...] * (1.0 + scale)
    if has_halo:
        xp_ref, x_ref, xn_ref = refs
        halo = jnp.concatenate([xp_ref[...], xn_ref[...]], axis=0)
        ubuf[t:t + 2 * HALO, :] = (_rms(halo) * gs + shift).astype(BF16)
    else:
        (x_ref,) = refs
    ubuf[0:t, :] = (_rms(x_ref[...]) * gs + shift).astype(BF16)
    return x_ref


def _edge_flags(tiles_per_seq):
    i = pl.program_id(0)
    pos = lax.rem(i, tiles_per_seq)
    return (pos != 0).astype(F32), (pos != tiles_per_seq - 1).astype(F32)


def _row_masks(t, rows_per_seq):
    pos = lax.rem(lax.broadcasted_iota(jnp.int32, (t, CONV_CHUNK), 0), rows_per_seq)
    return pos == 0, pos == rows_per_seq - 1


def _conv3(h, k, t, has_halo, flags, first_row, last_row):
    zm = h[0:t]
    if has_halo:
        zp = h[t + HALO - 1:t + HALO] * flags[0]
        zn = h[t + HALO:t + HALO + 1] * flags[1]
    else:
        zp = jnp.zeros((1, h.shape[1]), F32)
        zn = zp
    z_prev = jnp.where(first_row, zp, pltpu.roll(zm, 1, 0))
    z_next = jnp.where(last_row, zn, pltpu.roll(zm, t - 1, 0))
    return z_prev * k[0:1] + zm * k[1:2] + z_next * k[2:3]


def _finish(x_ref, y, mod_ref, gpost_ref, o_ref, d, gate_off):
    gate = mod_ref[0, :, gate_off * d:(gate_off + 1) * d]
    o_ref[...] = x_ref[...] + (gate * gpost_ref[...]) * _rms(y)


def _mixer_kernel(*refs, t, d, has_halo, tiles_per_seq, rows_per_seq):
    n_x = 3 if has_halo else 1
    x_refs = refs[:n_x]
    mod_ref, gpre_ref, gpost_ref, win_ref, ck_ref, wout_ref, o_ref, ubuf, ybuf = refs[n_x:]
    x_ref = _fill_u(x_refs, has_halo, mod_ref, gpre_ref, ubuf, t, d, 0)
    flags = _edge_flags(tiles_per_seq) if has_halo else None
    first_row, last_row = _row_masks(t, rows_per_seq)
    for j in range(d // CONV_CHUNK):
        c0 = j * CONV_CHUNK
        b_gate = _dot(ubuf[0:t, :], win_ref[:, c0:c0 + CONV_CHUNK])
        c_gate = _dot(ubuf[...], win_ref[:, d + c0:d + c0 + CONV_CHUNK])
        xp = _dot(ubuf[...], win_ref[:, 2 * d + c0:2 * d + c0 + CONV_CHUNK])
        conv = _conv3(c_gate * xp, ck_ref[:, c0:c0 + CONV_CHUNK], t, has_halo, flags, first_row, last_row)
        ybuf[:, c0:c0 + CONV_CHUNK] = (b_gate * conv).astype(BF16)
    y = _dot(ybuf[...], wout_ref[...])
    _finish(x_ref, y, mod_ref, gpost_ref, o_ref, d, 2)


def _ffn_kernel(*refs, t, d, d_ff, has_halo, tiles_per_seq, rows_per_seq):
    n_x = 3 if has_halo else 1
    x_refs = refs[:n_x]
    mod_ref, gpre_ref, gpost_ref, wup_ref, ck_ref, wdown_ref, o_ref, ubuf, abuf = refs[n_x:]
    x_ref = _fill_u(x_refs, has_halo, mod_ref, gpre_ref, ubuf, t, d, 3)
    flags = _edge_flags(tiles_per_seq) if has_halo else None
    first_row, last_row = _row_masks(t, rows_per_seq)
    for j in range(d_ff // CONV_CHUNK):
        c0 = j * CONV_CHUNK
        hg = _dot(ubuf[...], wup_ref[:, c0:c0 + CONV_CHUNK])
        hu = _dot(ubuf[...], wup_ref[:, d_ff + c0:d_ff + c0 + CONV_CHUNK])
        g = _conv3(hg, ck_ref[:, c0:c0 + CONV_CHUNK], t, has_halo, flags, first_row, last_row)
        u = _conv3(hu, ck_ref[:, d_ff + c0:d_ff + c0 + CONV_CHUNK], t, has_halo, flags, first_row, last_row)
        act = g * (1.0 / (1.0 + jnp.exp(-g))) * u
        abuf[:, c0:c0 + CONV_CHUNK] = act.astype(BF16)
    y = _dot(abuf[...], wdown_ref[...])
    _finish(x_ref, y, mod_ref, gpost_ref, o_ref, d, 5)


def _conv_sublayer(kernel_fn, name, layer, x, mod, gpre, gpost, w_a, ck, w_b, seq_len, t):
    rows, d = x.shape
    assert rows % t == 0 and t % V7X_BF16_ROWS == 0 and (seq_len % t == 0 or t % seq_len == 0)
    tiles_per_seq = max(seq_len // t, 1)
    has_halo = t < seq_len
    per_seq_mod = mod.shape[0] > 1
    assert not per_seq_mod or t <= seq_len
    hb = t // HALO
    last_hb = rows // HALO - 1

    def mod_map(i):
        return ((i // tiles_per_seq) if per_seq_mod else 0, 0, 0)

    x_specs = [pl.BlockSpec((t, d), lambda i: (i, 0))]
    x_args = [x]
    if has_halo:
        x_specs = [pl.BlockSpec((HALO, d), lambda i: (jnp.maximum(i * hb - 1, 0), 0)),
                   x_specs[0],
                   pl.BlockSpec((HALO, d), lambda i: (jnp.minimum((i + 1) * hb, last_hb), 0))]
        x_args = [x, x, x]
    m_rows = t + 2 * HALO if has_halo else t
    body = functools.partial(kernel_fn, t=t, d=d, has_halo=has_halo, tiles_per_seq=tiles_per_seq,
                             rows_per_seq=min(t, seq_len))
    return pl.pallas_call(
        body,
        grid=(rows // t,),
        in_specs=x_specs + [
            pl.BlockSpec((1, 1, mod.shape[2]), mod_map),
            _layer_spec(gpre, layer[0]), _layer_spec(gpost, layer[0]),
            _layer_spec(w_a, layer[1]), _layer_spec(ck, layer[1]), _layer_spec(w_b, layer[1]),
        ],
        out_specs=pl.BlockSpec((t, d), lambda i: (i, 0)),
        out_shape=jax.ShapeDtypeStruct((rows, d), F32),
        scratch_shapes=[pltpu.VMEM((m_rows, d), BF16), pltpu.VMEM((t, w_b.shape[1]), BF16)],
        compiler_params=pltpu.CompilerParams(dimension_semantics=("arbitrary",),
                                             vmem_limit_bytes=V7X_VMEM_LIMIT_BYTES),
        name=name,
    )(*x_args, mod, gpre, gpost, w_a, ck, w_b)


def _mixer(name, layer, x, mod, gpre, gpost, w_in, ck, w_out, seq_len, t):
    return _conv_sublayer(_mixer_kernel, name, layer, x, mod, gpre, gpost, w_in, ck, w_out, seq_len, t)


def _ffn(name, layer, x, mod, gpre, gpost, w_up, ck, w_down, seq_len, t):
    kernel_fn = functools.partial(_ffn_kernel, d_ff=w_down.shape[1])
    return _conv_sublayer(kernel_fn, name, layer, x, mod, gpre, gpost, w_up, ck, w_down, seq_len, t)


def _qkv_kernel(*refs, d, n_heads, rope, emit_f32_kv):
    x_ref, mod_ref, gpre_ref, w_ref, qg_ref, kg_ref = refs[:6]
    refs = refs[6:]
    if rope:
        cos_ref, sina_ref, sinb_ref = refs[:3]
        refs = refs[3:]
    q_ref, k_ref, v_ref = refs[:3]
    kv32_refs = refs[3:]
    t = x_ref.shape[0]
    shift = mod_ref[0, :, 0:d]
    scale = mod_ref[0, :, d:2 * d]
    u = (_rms(x_ref[...]) * (gpre_ref[...] * (1.0 + scale)) + shift).astype(BF16)
    qkv = _dot(u, w_ref[...])
    q_scale = HEAD_DIM ** -0.5 * LOG2_E

    def head(idx, gain):
        hv = _rms(qkv[:, idx * HEAD_DIM:(idx + 1) * HEAD_DIM]) * gain
        if rope:
            hv = (hv * cos_ref[...]
                  + pltpu.roll(hv, HEAD_DIM - HEAD_DIM // 4, 1) * sina_ref[...]
                  + pltpu.roll(hv, HEAD_DIM // 4, 1) * sinb_ref[...])
        return hv

    for hq in range(n_heads):
        q_ref[:, hq * HEAD_DIM:(hq + 1) * HEAD_DIM] = (head(hq, qg_ref[...]) * q_scale).astype(BF16)
    for hk in range(N_KV_HEADS):
        lanes = slice(hk * HEAD_DIM, (hk + 1) * HEAD_DIM)
        kh = head(n_heads + hk, kg_ref[...])
        vh = qkv[:, (n_heads + N_KV_HEADS + hk) * HEAD_DIM:(n_heads + N_KV_HEADS + hk + 1) * HEAD_DIM]
        k_ref[:, lanes] = kh.astype(BF16)
        v_ref[:, lanes] = vh.astype(BF16)
        if emit_f32_kv:
            kv32_refs[0][pl.ds(hk, t, stride=N_KV_HEADS), :] = kh
            kv32_refs[1][pl.ds(hk, t, stride=N_KV_HEADS), :] = vh


def _qkv(name, layer, x, mod, gpre, w_qkv, q_gain, k_gain, seq_len, t, rope_tables=None, emit_f32_kv=False):
    rows, d = x.shape
    n_heads = d // HEAD_DIM
    kv_w = N_KV_HEADS * HEAD_DIM
    tiles_per_seq = max(seq_len // t, 1)
    per_seq_mod = mod.shape[0] > 1
    rope = rope_tables is not None
    assert not (per_seq_mod or rope) or seq_len % t == 0

    def mod_map(i):
        return ((i // tiles_per_seq) if per_seq_mod else 0, 0, 0)

    in_specs = [
        pl.BlockSpec((t, d), lambda i: (i, 0)),
        pl.BlockSpec((1, 1, mod.shape[2]), mod_map),
        _layer_spec(gpre, layer[0]), _layer_spec(w_qkv, layer[1]),
        _layer_spec(q_gain, layer[1]), _layer_spec(k_gain, layer[1]),
    ]
    args = [x, mod, gpre, w_qkv, q_gain, k_gain]
    if rope:
        in_specs += [pl.BlockSpec((t, HEAD_DIM), lambda i: (i % tiles_per_seq, 0))] * 3
        args += list(rope_tables)
    out_specs = [pl.BlockSpec((t, d), lambda i: (i, 0)),
                 pl.BlockSpec((t, kv_w), lambda i: (i, 0)),
                 pl.BlockSpec((t, kv_w), lambda i: (i, 0))]
    out_shape = [jax.ShapeDtypeStruct((rows, d), BF16),
                 jax.ShapeDtypeStruct((rows, kv_w), BF16),
                 jax.ShapeDtypeStruct((rows, kv_w), BF16)]
    if emit_f32_kv:
        out_specs += [pl.BlockSpec((N_KV_HEADS * t, HEAD_DIM), lambda i: (i, 0))] * 2
        out_shape += [jax.ShapeDtypeStruct((N_KV_HEADS * rows, HEAD_DIM), F32)] * 2
    return pl.pallas_call(
        functools.partial(_qkv_kernel, d=d, n_heads=n_heads, rope=rope, emit_f32_kv=emit_f32_kv),
        grid=(rows // t,),
        in_specs=in_specs,
        out_specs=out_specs,
        out_shape=out_shape,
        compiler_params=pltpu.CompilerParams(dimension_semantics=("arbitrary",),
                                             vmem_limit_bytes=V7X_VMEM_LIMIT_BYTES),
        name=name,
    )(*args)


def _kv_block(ref, kk, b0, kb):
    if ref.shape[1] == HEAD_DIM:
        return ref[pl.ds(N_KV_HEADS * b0 + kk, kb, stride=N_KV_HEADS), :].astype(BF16)
    return ref[b0:b0 + kb, kk * HEAD_DIM:(kk + 1) * HEAD_DIM]


def _attn_kernel(*refs, d, n_heads, n_kv_sets):
    x_ref, q_ref = refs[:2]
    kv_refs = refs[2:2 + 2 * n_kv_sets]
    mod_ref, gpost_ref, wo_ref, o_ref, obuf = refs[2 + 2 * n_kv_sets:]
    tq = q_ref.shape[0]
    group = n_heads // N_KV_HEADS
    for kk in range(N_KV_HEADS):
        qs = jnp.concatenate(
            [q_ref[:, (kk * group + g) * HEAD_DIM:(kk * group + g + 1) * HEAD_DIM] for g in range(group)], axis=0)
        m = denom = acc = None
        for s in range(n_kv_sets):
            k_ref, v_ref = kv_refs[2 * s], kv_refs[2 * s + 1]
            n_keys = k_ref.shape[0] * k_ref.shape[1] // (N_KV_HEADS * HEAD_DIM)
            kb = min(KEY_BLOCK, n_keys)
            for b0 in range(0, n_keys, kb):
                sc = _dot_nt(qs, _kv_block(k_ref, kk, b0, kb))
                v_blk = _kv_block(v_ref, kk, b0, kb)
                mb = sc.max(axis=-1, keepdims=True)
                if m is None:
                    m = mb
                    p = jnp.exp2(sc - m)
                    denom = p.sum(axis=-1, keepdims=True)
                    acc = _dot(p.astype(BF16), v_blk)
                else:
                    m_new = jnp.maximum(m, mb)
                    alpha = jnp.exp2(m - m_new)
                    p = jnp.exp2(sc - m_new)
                    denom = alpha * denom + p.sum(axis=-1, keepdims=True)
                    acc = alpha * acc + _dot(p.astype(BF16), v_blk)
                    m = m_new
        out = acc * (1.0 / denom)
        for g in range(group):
            hq = kk * group + g
            obuf[:, hq * HEAD_DIM:(hq + 1) * HEAD_DIM] = out[g * tq:(g + 1) * tq].astype(BF16)
    y = _dot(obuf[...], wo_ref[...])
    _finish(x_ref, y, mod_ref, gpost_ref, o_ref, d, 2)


def _attention(name, layer, x, q, kv_sets, mod, gpost, w_o, seq_len, tq):
    rows, d = x.shape
    n_heads = d // HEAD_DIM
    kv_w = N_KV_HEADS * HEAD_DIM
    tiles_per_seq = seq_len // tq
    per_seq_mod = mod.shape[0] > 1

    def mod_map(i):
        return ((i // tiles_per_seq) if per_seq_mod else 0, 0, 0)

    in_specs = [pl.BlockSpec((tq, d), lambda i: (i, 0)), pl.BlockSpec((tq, d), lambda i: (i, 0))]
    args = [x, q]
    for k, v, n in kv_sets:
        block = (n * kv_w // k.shape[1], k.shape[1])
        in_specs += [pl.BlockSpec(block, lambda i: (i // tiles_per_seq, 0))] * 2
        args += [k, v]
    in_specs += [pl.BlockSpec((1, 1, mod.shape[2]), mod_map),
                 _layer_spec(gpost, layer[0]), _layer_spec(w_o, layer[1])]
    args += [mod, gpost, w_o]
    return pl.pallas_call(
        functools.partial(_attn_kernel, d=d, n_heads=n_heads, n_kv_sets=len(kv_sets)),
        grid=(rows // tq,),
        in_specs=in_specs,
        out_specs=pl.BlockSpec((tq, d), lambda i: (i, 0)),
        out_shape=jax.ShapeDtypeStruct((rows, d), F32),
        scratch_shapes=[pltpu.VMEM((tq, d), BF16)],
        compiler_params=pltpu.CompilerParams(dimension_semantics=("arbitrary",),
                                             vmem_limit_bytes=V7X_VMEM_LIMIT_BYTES),
        name=name,
    )(*args)


def _rope_tables(n_lat):
    half = HEAD_DIM // 2
    rows_n = n_lat // GRID_W
    row = np.repeat(np.arange(rows_n), GRID_W).astype(np.float32)
    col = np.tile(np.arange(GRID_W), rows_n).astype(np.float32)
    inv = (np.float32(ROPE_THETA) ** (-np.arange(0, half, 2, dtype=np.float32) / np.float32(half))).astype(np.float32)
    ang_r = row[:, None] * inv[None, :]
    ang_c = col[:, None] * inv[None, :]
    cr, sr, cc, sc = np.cos(ang_r), np.sin(ang_r), np.cos(ang_c), np.sin(ang_c)
    zero = np.zeros_like(sr)
    cos = np.concatenate([cr, cr, cc, cc], axis=-1)
    sin_a = np.concatenate([-sr, zero, -sc, zero], axis=-1)
    sin_b = np.concatenate([zero, sr, zero, sc], axis=-1)
    return jnp.asarray(cos), jnp.asarray(sin_a), jnp.asarray(sin_b)


def kernel(x_prompt, x_sample, cache_k, cache_v, c, c_ctx, mod_w, mod_b, norm_mix_pre, norm_mix_post, norm_ffn_pre, norm_ffn_post, conv_w_in, conv_k, conv_w_out, attn_w_qkv, attn_q_gain, attn_k_gain, attn_w_o, ffn_w_up, ffn_conv, ffn_w_down):
    batch, seq, d = x_prompt.shape
    dec_batch, dec_seq, _ = x_sample.shape
    depth = mod_w.shape[0]
    past_len = cache_k.shape[2]
    t_ctx = 2 * seq
    tq_ctx = seq
    t_lat = 512
    tq_lat = 256

    n_cond = dec_batch + 1
    pad = (-n_cond) % V7X_BF16_ROWS
    cvec = jnp.concatenate([c, c_ctx[None, :], jnp.zeros((pad, d), F32)], axis=0)
    mod_all = _modulation(cvec, mod_w, mod_b)

    per_layer_rows = lambda a: a.reshape(a.shape[0], 1, a.shape[1])
    g_mix_pre, g_mix_post = per_layer_rows(norm_mix_pre), per_layer_rows(norm_mix_post)
    g_ffn_pre, g_ffn_post = per_layer_rows(norm_ffn_pre), per_layer_rows(norm_ffn_post)
    q_gain, k_gain = per_layer_rows(attn_q_gain), per_layer_rows(attn_k_gain)
    w_in, w_out = conv_w_in.astype(BF16), conv_w_out.astype(BF16)
    w_qkv, w_o = attn_w_qkv.astype(BF16), attn_w_o.astype(BF16)
    w_up, w_down = ffn_w_up.astype(BF16), ffn_w_down.astype(BF16)

    h_ctx = x_prompt.reshape(batch * seq, d)
    h_lat = x_sample.reshape(dec_batch * dec_seq, d)
    rope_tables = _rope_tables(dec_seq)
    new_k, new_v = [], []

    for i in range(depth):
        j = i // 2
        mod_lat = mod_all[i, :dec_batch][:, None, :]
        mod_ctx = mod_all[i, dec_batch:dec_batch + 1][:, None, :]
        if i % 2 == 0:
            h_ctx = _mixer(f"mixer{i}_ctx", (i, j), h_ctx, mod_ctx, g_mix_pre, g_mix_post, w_in, conv_k, w_out, seq, t_ctx)
            h_lat = _mixer(f"mixer{i}_lat", (i, j), h_lat, mod_lat, g_mix_pre, g_mix_post, w_in, conv_k, w_out, dec_seq, t_lat)
        else:
            q_c, k_c, v_c, k_c32, v_c32 = _qkv(f"qkv{i}_ctx", (i, j), h_ctx, mod_ctx, g_mix_pre, w_qkv, q_gain, k_gain,
                                               seq, t_ctx, emit_f32_kv=True)
            new_k.append(k_c32.reshape(batch, seq, N_KV_HEADS, HEAD_DIM))
            new_v.append(v_c32.reshape(batch, seq, N_KV_HEADS, HEAD_DIM))
            h_ctx = _attention(f"attn{i}_ctx", (i, j), h_ctx, q_c, [(k_c, v_c, seq)], mod_ctx, g_mix_post, w_o, seq, tq_ctx)
            q_l, k_l, v_l = _qkv(f"qkv{i}_lat", (i, j), h_lat, mod_lat, g_mix_pre, w_qkv, q_gain, k_gain,
                                 dec_seq, t_lat, rope_tables=rope_tables)
            ck = cache_k[:, j].reshape(dec_batch * past_len * N_KV_HEADS, HEAD_DIM)
            cv = cache_v[:, j].reshape(dec_batch * past_len * N_KV_HEADS, HEAD_DIM)
            h_lat = _attention(f"attn{i}_lat", (i, j), h_lat, q_l, [(ck, cv, past_len), (k_l, v_l, dec_seq)],
                               mod_lat, g_mix_post, w_o, dec_seq, tq_lat)
        h_ctx = _ffn(f"ffn{i}_ctx", (i, i), h_ctx, mod_ctx, g_ffn_pre, g_ffn_post, w_up, ffn_conv, w_down, seq, t_ctx)
        h_lat = _ffn(f"ffn{i}_lat", (i, i), h_lat, mod_lat, g_ffn_pre, g_ffn_post, w_up, ffn_conv, w_down, dec_seq, t_lat)

    return (h_ctx.reshape(batch, seq, d), h_lat.reshape(dec_batch, dec_seq, d),
            jnp.stack(new_k, axis=1), jnp.stack(new_v, axis=1))
```

```python
import functools

import jax
import jax.numpy as jnp
import numpy as np
from jax import lax
from jax.experimental import pallas as pl
from jax.experimental.pallas import tpu as pltpu

EPS = 1e-6
ROPE_THETA = 10000.0
GRID_W = 64
HEAD_DIM = 128
N_KV_HEADS = 2

V7X_SUBLANES = 8
V7X_BF16_ROWS = 16
V7X_VMEM_LIMIT_BYTES = 56 * 1024 * 1024

HALO = V7X_SUBLANES
CONV_CHUNK = 256
KEY_BLOCK = 1024
LOG2_E = 1.4426950408889634

BF16 = jnp.bfloat16
F32 = jnp.float32


def _resident(block_shape, index_map):
    return pl.BlockSpec(block_shape, index_map, pipeline_mode=pl.Buffered(1))


def _layer_spec(stacked, layer):
    zeros = (0,) * (stacked.ndim - 1)
    return _resident((None,) + stacked.shape[1:], lambda i: (layer,) + zeros)


def _with_side_casts(kernel_fn, n_in, n_out, n_casts):
    if n_casts == 0:
        return kernel_fn

    def body(*refs):
        ins, rest = refs[:n_in], refs[n_in:]
        srcs, rest = rest[:n_casts], rest[n_casts:]
        outs, rest = rest[:n_out], rest[n_out:]
        dsts, scratch = rest[:n_casts], rest[n_casts:]
        for src, dst in zip(srcs, dsts):
            dst[...] = src[...].astype(BF16)
        kernel_fn(*ins, *outs, *scratch)

    return body


def _cast_specs(casts, steps):
    in_specs, out_specs, out_shape = [], [], []
    for stacked, layer in casts:
        _, rows, cols = stacked.shape
        assert rows % (steps * V7X_BF16_ROWS) == 0
        blk = rows // steps
        in_specs.append(pl.BlockSpec((None, blk, cols), lambda i, layer=layer: (layer, i, 0)))
        out_specs.append(pl.BlockSpec((blk, cols), lambda i: (i, 0)))
        out_shape.append(jax.ShapeDtypeStruct((rows, cols), BF16))
    return in_specs, out_specs, out_shape


def _rms(x):
    return x * lax.rsqrt(jnp.mean(x * x, axis=-1, keepdims=True) + EPS)


def _dot(a, b):
    return jnp.dot(a, b, preferred_element_type=F32)


def _dot_nt(a, b):
    return lax.dot_general(a, b, (((1,), (1,)), ((), ())), preferred_element_type=F32)


def _mod_kernel(c_ref, w_ref, b_ref, o_ref):
    c = c_ref[...]
    s = c * (1.0 / (1.0 + jnp.exp(-c)))
    o_ref[0] = _dot(s.astype(BF16), w_ref[0].astype(BF16)) + b_ref[0]


def _modulation(cvec, mod_w, mod_b):
    depth, d, n = mod_w.shape
    rows = cvec.shape[0]
    tn = 1536
    return pl.pallas_call(
        _mod_kernel,
        grid=(depth, n // tn),
        in_specs=[
            pl.BlockSpec((rows, d), lambda l, j: (0, 0)),
            pl.BlockSpec((1, d, tn), lambda l, j: (l, 0, j)),
            pl.BlockSpec((1, 1, tn), lambda l, j: (l, 0, j)),
        ],
        out_specs=pl.BlockSpec((1, rows, tn), lambda l, j: (l, 0, j)),
        out_shape=jax.ShapeDtypeStruct((depth, rows, n), F32),
        compiler_params=pltpu.CompilerParams(dimension_semantics=("arbitrary", "arbitrary")),
        name="modulation",
    )(cvec, mod_w, mod_b.reshape(depth, 1, n))


def _fill_u(refs, has_halo, mod_ref, gpre_ref, ubuf, t, d, mod_off):
    shift = mod_ref[0, :, mod_off * d:(mod_off + 1) * d]
    scale = mod_ref[0, :, (mod_off + 1) * d:(mod_off + 2) * d]
    gs = gpre_ref[...] * (1.0 + scale)
    if has_halo:
        xp_ref, x_ref, xn_ref = refs
        halo = jnp.concatenate([xp_ref[...], xn_ref[...]], axis=0)
        ubuf[t:t + 2 * HALO, :] = (_rms(halo) * gs + shift).astype(BF16)
    else:
        (x_ref,) = refs
    ubuf[0:t, :] = (_rms(x_ref[...]) * gs + shift).astype(BF16)
    return x_ref


def _edge_flags(tiles_per_seq):
    i = pl.program_id(0)
    pos = lax.rem(i, tiles_per_seq)
    return (pos != 0).astype(F32), (pos != tiles_per_seq - 1).astype(F32)


def _row_masks(t, rows_per_seq):
    pos = lax.rem(lax.broadcasted_iota(jnp.int32, (t, CONV_CHUNK), 0), rows_per_seq)
    return pos == 0, pos == rows_per_seq - 1


def _conv3(h, k, t, has_halo, flags, first_row, last_row):
    zm = h[0:t]
    if has_halo:
        zp = h[t + HALO - 1:t + HALO] * flags[0]
        zn = h[t + HALO:t + HALO + 1] * flags[1]
    else:
        zp = jnp.zeros((1, h.shape[1]), F32)
        zn = zp
    z_prev = jnp.where(first_row, zp, pltpu.roll(zm, 1, 0))
    z_next = jnp.where(last_row, zn, pltpu.roll(zm, t - 1, 0))
    return z_prev * k[0:1] + zm * k[1:2] + z_next * k[2:3]


def _finish(x_ref, y, mod_ref, gpost_ref, o_ref, d, gate_off):
    gate = mod_ref[0, :, gate_off * d:(gate_off + 1) * d]
    o_ref[...] = x_ref[...] + (gate * gpost_ref[...]) * _rms(y)


def _mixer_kernel(*refs, t, d, has_halo, tiles_per_seq, rows_per_seq):
    n_x = 3 if has_halo else 1
    x_refs = refs[:n_x]
    mod_ref, gpre_ref, gpost_ref, win_ref, ck_ref, wout_ref, o_ref, ubuf, ybuf = refs[n_x:]
    x_ref = _fill_u(x_refs, has_halo, mod_ref, gpre_ref, ubuf, t, d, 0)
    flags = _edge_flags(tiles_per_seq) if has_halo else None
    first_row, last_row = _row_masks(t, rows_per_seq)
    for j in range(d // CONV_CHUNK):
        c0 = j * CONV_CHUNK
        b_gate = _dot(ubuf[0:t, :], win_ref[:, c0:c0 + CONV_CHUNK])
        c_gate = _dot(ubuf[...], win_ref[:, d + c0:d + c0 + CONV_CHUNK])
        xp = _dot(ubuf[...], win_ref[:, 2 * d + c0:2 * d + c0 + CONV_CHUNK])
        conv = _conv3(c_gate * xp, ck_ref[:, c0:c0 + CONV_CHUNK], t, has_halo, flags, first_row, last_row)
        ybuf[:, c0:c0 + CONV_CHUNK] = (b_gate * conv).astype(BF16)
    y = _dot(ybuf[...], wout_ref[...])
    _finish(x_ref, y, mod_ref, gpost_ref, o_ref, d, 2)


def _ffn_kernel(*refs, t, d, d_ff, has_halo, tiles_per_seq, rows_per_seq):
    n_x = 3 if has_halo else 1
    x_refs = refs[:n_x]
    mod_ref, gpre_ref, gpost_ref, wup_ref, ck_ref, wdown_ref, o_ref, ubuf, abuf = refs[n_x:]
    x_ref = _fill_u(x_refs, has_halo, mod_ref, gpre_ref, ubuf, t, d, 3)
    flags = _edge_flags(tiles_per_seq) if has_halo else None
    first_row, last_row = _row_masks(t, rows_per_seq)
    for j in range(d_ff // CONV_CHUNK):
        c0 = j * CONV_CHUNK
        hg = _dot(ubuf[...], wup_ref[:, c0:c0 + CONV_CHUNK])
        hu = _dot(ubuf[...], wup_ref[:, d_ff + c0:d_ff + c0 + CONV_CHUNK])
        g = _conv3(hg, ck_ref[:, c0:c0 + CONV_CHUNK], t, has_halo, flags, first_row, last_row)
        u = _conv3(hu, ck_ref[:, d_ff + c0:d_ff + c0 + CONV_CHUNK], t, has_halo, flags, first_row, last_row)
        act = g * (1.0 / (1.0 + jnp.exp(-g))) * u
        abuf[:, c0:c0 + CONV_CHUNK] = act.astype(BF16)
    y = _dot(abuf[...], wdown_ref[...])
    _finish(x_ref, y, mod_ref, gpost_ref, o_ref, d, 5)


def _conv_sublayer(kernel_fn, name, layer, x, mod, gpre, gpost, w_a, ck, w_b, seq_len, t, casts=()):
    rows, d = x.shape
    assert rows % t == 0 and t % V7X_BF16_ROWS == 0 and (seq_len % t == 0 or t % seq_len == 0)
    tiles_per_seq = max(seq_len // t, 1)
    has_halo = t < seq_len
    per_seq_mod = mod.shape[0] > 1
    assert not per_seq_mod or t <= seq_len
    hb = t // HALO
    last_hb = rows // HALO - 1

    def mod_map(i):
        return ((i // tiles_per_seq) if per_seq_mod else 0, 0, 0)

    x_specs = [pl.BlockSpec((t, d), lambda i: (i, 0))]
    x_args = [x]
    if has_halo:
        x_specs = [pl.BlockSpec((HALO, d), lambda i: (jnp.maximum(i * hb - 1, 0), 0)),
                   x_specs[0],
                   pl.BlockSpec((HALO, d), lambda i: (jnp.minimum((i + 1) * hb, last_hb), 0))]
        x_args = [x, x, x]
    m_rows = t + 2 * HALO if has_halo else t
    body = functools.partial(kernel_fn, t=t, d=d, has_halo=has_halo, tiles_per_seq=tiles_per_seq,
                             rows_per_seq=min(t, seq_len))
    in_specs = x_specs + [
        pl.BlockSpec((1, 1, mod.shape[2]), mod_map),
        _layer_spec(gpre, layer[0]), _layer_spec(gpost, layer[0]),
        _layer_spec(w_a, layer[1]), _layer_spec(ck, layer[2]), _layer_spec(w_b, layer[1]),
    ]
    cast_in, cast_out, cast_shape = _cast_specs(casts, rows // t)
    out = pl.pallas_call(
        _with_side_casts(body, len(in_specs), 1, len(casts)),
        grid=(rows // t,),
        in_specs=in_specs + cast_in,
        out_specs=[pl.BlockSpec((t, d), lambda i: (i, 0))] + cast_out,
        out_shape=[jax.ShapeDtypeStruct((rows, d), F32)] + cast_shape,
        scratch_shapes=[pltpu.VMEM((m_rows, d), BF16), pltpu.VMEM((t, w_b.shape[1]), BF16)],
        compiler_params=pltpu.CompilerParams(dimension_semantics=("arbitrary",),
                                             vmem_limit_bytes=V7X_VMEM_LIMIT_BYTES),
        name=name,
    )(*x_args, mod, gpre, gpost, w_a, ck, w_b, *[a for a, _ in casts])
    return out[0], out[1:]


def _mixer(name, layer, x, mod, gpre, gpost, w_in, ck, w_out, seq_len, t, casts=()):
    return _conv_sublayer(_mixer_kernel, name, layer, x, mod, gpre, gpost, w_in, ck, w_out, seq_len, t, casts)


def _ffn(name, layer, x, mod, gpre, gpost, w_up, ck, w_down, seq_len, t):
    kernel_fn = functools.partial(_ffn_kernel, d_ff=w_down.shape[1])
    return _conv_sublayer(kernel_fn, name, layer, x, mod, gpre, gpost, w_up, ck, w_down, seq_len, t)[0]


def _qkv_kernel(*refs, d, n_heads, rope, emit_f32_kv):
    x_ref, mod_ref, gpre_ref, w_ref, qg_ref, kg_ref = refs[:6]
    refs = refs[6:]
    if rope:
        cos_ref, sina_ref, sinb_ref = refs[:3]
        refs = refs[3:]
    q_ref, k_ref, v_ref = refs[:3]
    kv32_refs = refs[3:]
    t = x_ref.shape[0]
    shift = mod_ref[0, :, 0:d]
    scale = mod_ref[0, :, d:2 * d]
    u = (_rms(x_ref[...]) * (gpre_ref[...] * (1.0 + scale)) + shift).astype(BF16)
    qkv = _dot(u, w_ref[...])
    q_scale = HEAD_DIM ** -0.5 * LOG2_E

    def head(idx, gain):
        hv = _rms(qkv[:, idx * HEAD_DIM:(idx + 1) * HEAD_DIM]) * gain
        if rope:
            hv = (hv * cos_ref[...]
                  + pltpu.roll(hv, HEAD_DIM - HEAD_DIM // 4, 1) * sina_ref[...]
                  + pltpu.roll(hv, HEAD_DIM // 4, 1) * sinb_ref[...])
        return hv

    for hq in range(n_heads):
        q_ref[:, hq * HEAD_DIM:(hq + 1) * HEAD_DIM] = (head(hq, qg_ref[...]) * q_scale).astype(BF16)
    for hk in range(N_KV_HEADS):
        lanes = slice(hk * HEAD_DIM, (hk + 1) * HEAD_DIM)
        kh = head(n_heads + hk, kg_ref[...])
        vh = qkv[:, (n_heads + N_KV_HEADS + hk) * HEAD_DIM:(n_heads + N_KV_HEADS + hk + 1) * HEAD_DIM]
        k_ref[:, lanes] = kh.astype(BF16)
        v_ref[:, lanes] = vh.astype(BF16)
        if emit_f32_kv:
            kv32_refs[0][pl.ds(hk, t, stride=N_KV_HEADS), :] = kh
            kv32_refs[1][pl.ds(hk, t, stride=N_KV_HEADS), :] = vh


def _qkv(name, layer, x, mod, gpre, w_qkv, q_gain, k_gain, seq_len, t, rope_tables=None, emit_f32_kv=False,
         casts=()):
    rows, d = x.shape
    n_heads = d // HEAD_DIM
    kv_w = N_KV_HEADS * HEAD_DIM
    tiles_per_seq = max(seq_len // t, 1)
    per_seq_mod = mod.shape[0] > 1
    rope = rope_tables is not None
    assert not (per_seq_mod or rope) or seq_len % t == 0

    def mod_map(i):
        return ((i // tiles_per_seq) if per_seq_mod else 0, 0, 0)

    in_specs = [
        pl.BlockSpec((t, d), lambda i: (i, 0)),
        pl.BlockSpec((1, 1, mod.shape[2]), mod_map),
        _layer_spec(gpre, layer[0]), _layer_spec(w_qkv, layer[1]),
        _layer_spec(q_gain, layer[1]), _layer_spec(k_gain, layer[1]),
    ]
    args = [x, mod, gpre, w_qkv, q_gain, k_gain]
    if rope:
        in_specs += [pl.BlockSpec((t, HEAD_DIM), lambda i: (i % tiles_per_seq, 0))] * 3
        args += list(rope_tables)
    out_specs = [pl.BlockSpec((t, d), lambda i: (i, 0)),
                 pl.BlockSpec((t, kv_w), lambda i: (i, 0)),
                 pl.BlockSpec((t, kv_w), lambda i: (i, 0))]
    out_shape = [jax.ShapeDtypeStruct((rows, d), BF16),
                 jax.ShapeDtypeStruct((rows, kv_w), BF16),
                 jax.ShapeDtypeStruct((rows, kv_w), BF16)]
    if emit_f32_kv:
        out_specs += [pl.BlockSpec((N_KV_HEADS * t, HEAD_DIM), lambda i: (i, 0))] * 2
        out_shape += [jax.ShapeDtypeStruct((N_KV_HEADS * rows, HEAD_DIM), F32)] * 2
    cast_in, cast_out, cast_shape = _cast_specs(casts, rows // t)
    body = functools.partial(_qkv_kernel, d=d, n_heads=n_heads, rope=rope, emit_f32_kv=emit_f32_kv)
    out = pl.pallas_call(
        _with_side_casts(body, len(in_specs), len(out_specs), len(casts)),
        grid=(rows // t,),
        in_specs=in_specs + cast_in,
        out_specs=out_specs + cast_out,
        out_shape=out_shape + cast_shape,
        compiler_params=pltpu.CompilerParams(dimension_semantics=("arbitrary",),
                                             vmem_limit_bytes=V7X_VMEM_LIMIT_BYTES),
        name=name,
    )(*args, *[a for a, _ in casts])
    return out[:len(out_specs)], out[len(out_specs):]


def _kv_block(ref, kk, b0, kb):
    if ref.shape[1] == HEAD_DIM:
        return ref[pl.ds(N_KV_HEADS * b0 + kk, kb, stride=N_KV_HEADS), :].astype(BF16)
    return ref[b0:b0 + kb, kk * HEAD_DIM:(kk + 1) * HEAD_DIM]


def _attn_kernel(*refs, d, n_heads, n_kv_sets, seqs_per_tile):
    x_ref, q_ref = refs[:2]
    kv_refs = refs[2:2 + 2 * n_kv_sets]
    mod_ref, gpost_ref, wo_ref, o_ref, obuf = refs[2 + 2 * n_kv_sets:]
    tq = q_ref.shape[0] // seqs_per_tile
    group = n_heads // N_KV_HEADS
    for sub, kk in [(a, b) for a in range(seqs_per_tile) for b in range(N_KV_HEADS)]:
        rows = slice(sub * tq, (sub + 1) * tq)
        qs = jnp.concatenate(
            [q_ref[rows, (kk * group + g) * HEAD_DIM:(kk * group + g + 1) * HEAD_DIM] for g in range(group)], axis=0)
        m = denom = acc = None
        for s in range(n_kv_sets):
            k_ref, v_ref = kv_refs[2 * s], kv_refs[2 * s + 1]
            n_keys = k_ref.shape[0] * k_ref.shape[1] // (N_KV_HEADS * HEAD_DIM * seqs_per_tile)
            kb = min(KEY_BLOCK, n_keys)
            for b0 in range(sub * n_keys, (sub + 1) * n_keys, kb):
                sc = _dot_nt(qs, _kv_block(k_ref, kk, b0, kb))
                v_blk = _kv_block(v_ref, kk, b0, kb)
                mb = sc.max(axis=-1, keepdims=True)
                if m is None:
                    m = mb
                    p = jnp.exp2(sc - m)
                    denom = p.sum(axis=-1, keepdims=True)
                    acc = _dot(p.astype(BF16), v_blk)
                else:
                    m_new = jnp.maximum(m, mb)
                    alpha = jnp.exp2(m - m_new)
                    p = jnp.exp2(sc - m_new)
                    denom = alpha * denom + p.sum(axis=-1, keepdims=True)
                    acc = alpha * acc + _dot(p.astype(BF16), v_blk)
                    m = m_new
        out = acc * (1.0 / denom)
        for g in range(group):
            hq = kk * group + g
            obuf[rows, hq * HEAD_DIM:(hq + 1) * HEAD_DIM] = out[g * tq:(g + 1) * tq].astype(BF16)
    y = _dot(obuf[...], wo_ref[...])
    _finish(x_ref, y, mod_ref, gpost_ref, o_ref, d, 2)


def _attention(name, layer, x, q, kv_sets, mod, gpost, w_o, seq_len, tq):
    rows, d = x.shape
    n_heads = d // HEAD_DIM
    kv_w = N_KV_HEADS * HEAD_DIM
    assert seq_len % tq == 0 or tq % seq_len == 0
    tiles_per_seq = max(seq_len // tq, 1)
    seqs_per_tile = max(tq // seq_len, 1)
    per_seq_mod = mod.shape[0] > 1
    assert not per_seq_mod or seqs_per_tile == 1

    def mod_map(i):
        return ((i // tiles_per_seq) if per_seq_mod else 0, 0, 0)

    in_specs = [pl.BlockSpec((tq, d), lambda i: (i, 0)), pl.BlockSpec((tq, d), lambda i: (i, 0))]
    args = [x, q]
    for k, v, n in kv_sets:
        block = (seqs_per_tile * n * kv_w // k.shape[1], k.shape[1])
        in_specs += [pl.BlockSpec(block, lambda i: (i // tiles_per_seq, 0))] * 2
        args += [k, v]
    in_specs += [pl.BlockSpec((1, 1, mod.shape[2]), mod_map),
                 _layer_spec(gpost, layer[0]), _layer_spec(w_o, layer[1])]
    args += [mod, gpost, w_o]
    return pl.pallas_call(
        functools.partial(_attn_kernel, d=d, n_heads=n_heads, n_kv_sets=len(kv_sets), seqs_per_tile=seqs_per_tile),
        grid=(rows // tq,),
        in_specs=in_specs,
        out_specs=pl.BlockSpec((tq, d), lambda i: (i, 0)),
        out_shape=jax.ShapeDtypeStruct((rows, d), F32),
        scratch_shapes=[pltpu.VMEM((tq, d), BF16)],
        compiler_params=pltpu.CompilerParams(dimension_semantics=("arbitrary",),
                                             vmem_limit_bytes=V7X_VMEM_LIMIT_BYTES),
        name=name,
    )(*args)


def _rope_tables(n_lat):
    half = HEAD_DIM // 2
    rows_n = n_lat // GRID_W
    row = np.repeat(np.arange(rows_n), GRID_W).astype(np.float32)
    col = np.tile(np.arange(GRID_W), rows_n).astype(np.float32)
    inv = (np.float32(ROPE_THETA) ** (-np.arange(0, half, 2, dtype=np.float32) / np.float32(half))).astype(np.float32)
    ang_r = row[:, None] * inv[None, :]
    ang_c = col[:, None] * inv[None, :]
    cr, sr, cc, sc = np.cos(ang_r), np.sin(ang_r), np.cos(ang_c), np.sin(ang_c)
    zero = np.zeros_like(sr)
    cos = np.concatenate([cr, cr, cc, cc], axis=-1)
    sin_a = np.concatenate([-sr, zero, -sc, zero], axis=-1)
    sin_b = np.concatenate([zero, sr, zero, sc], axis=-1)
    return jnp.asarray(cos), jnp.asarray(sin_a), jnp.asarray(sin_b)


def kernel(x_prompt, x_sample, cache_k, cache_v, c, c_ctx, mod_w, mod_b, norm_mix_pre, norm_mix_post, norm_ffn_pre, norm_ffn_post, conv_w_in, conv_k, conv_w_out, attn_w_qkv, attn_q_gain, attn_k_gain, attn_w_o, ffn_w_up, ffn_conv, ffn_w_down):
    batch, seq, d = x_prompt.shape
    dec_batch, dec_seq, _ = x_sample.shape
    depth = mod_w.shape[0]
    past_len = cache_k.shape[2]
    t_ctx = 2 * seq
    tq_ctx = 2 * seq
    t_lat_conv = 1024
    t_lat_qkv = 512
    tq_lat = 256

    n_cond = dec_batch + 1
    pad = (-n_cond) % V7X_BF16_ROWS
    cvec = jnp.concatenate([c, c_ctx[None, :], jnp.zeros((pad, d), F32)], axis=0)
    mod_all = _modulation(cvec, mod_w, mod_b)

    per_layer_rows = lambda a: a.reshape(a.shape[0], 1, a.shape[1])
    g_mix_pre, g_mix_post = per_layer_rows(norm_mix_pre), per_layer_rows(norm_mix_post)
    g_ffn_pre, g_ffn_post = per_layer_rows(norm_ffn_pre), per_layer_rows(norm_ffn_post)
    q_gain, k_gain = per_layer_rows(attn_q_gain), per_layer_rows(attn_k_gain)
    w_in, w_out = conv_w_in.astype(BF16), conv_w_out.astype(BF16)
    w_qkv, w_o = attn_w_qkv.astype(BF16), attn_w_o.astype(BF16)

    h_ctx = x_prompt.reshape(batch * seq, d)
    h_lat = x_sample.reshape(dec_batch * dec_seq, d)
    rope_tables = _rope_tables(dec_seq)
    new_k, new_v = [], []

    for i in range(depth):
        j = i // 2
        mod_lat = mod_all[i, :dec_batch][:, None, :]
        mod_ctx = mod_all[i, dec_batch:dec_batch + 1][:, None, :]
        if i % 2 == 0:
            h_ctx, (w_down,) = _mixer(f"mixer{i}_ctx", (i, j, j), h_ctx, mod_ctx, g_mix_pre, g_mix_post, w_in, conv_k,
                                      w_out, seq, t_ctx, casts=[(ffn_w_down, i)])
            h_lat, (w_up,) = _mixer(f"mixer{i}_lat", (i, j, j), h_lat, mod_lat, g_mix_pre, g_mix_post, w_in, conv_k,
                                    w_out, dec_seq, t_lat_conv, casts=[(ffn_w_up, i)])
        else:
            (q_c, k_c, v_c, k_c32, v_c32), (w_up, w_down) = _qkv(
                f"qkv{i}_ctx", (i, j), h_ctx, mod_ctx, g_mix_pre, w_qkv, q_gain, k_gain, seq, t_ctx,
                emit_f32_kv=True, casts=[(ffn_w_up, i), (ffn_w_down, i)])
            new_k.append(k_c32.reshape(batch, seq, N_KV_HEADS, HEAD_DIM))
            new_v.append(v_c32.reshape(batch, seq, N_KV_HEADS, HEAD_DIM))
            h_ctx = _attention(f"attn{i}_ctx", (i, j), h_ctx, q_c, [(k_c, v_c, seq)], mod_ctx, g_mix_post, w_o, seq, tq_ctx)
            (q_l, k_l, v_l), _ = _qkv(f"qkv{i}_lat", (i, j), h_lat, mod_lat, g_mix_pre, w_qkv, q_gain, k_gain,
                                      dec_seq, t_lat_qkv, rope_tables=rope_tables)
            ck = cache_k[:, j].reshape(dec_batch * past_len * N_KV_HEADS, HEAD_DIM)
            cv = cache_v[:, j].reshape(dec_batch * past_len * N_KV_HEADS, HEAD_DIM)
            h_lat = _attention(f"attn{i}_lat", (i, j), h_lat, q_l, [(ck, cv, past_len), (k_l, v_l, dec_seq)],
                               mod_lat, g_mix_post, w_o, dec_seq, tq_lat)
        h_ctx = _ffn(f"ffn{i}_ctx", (i, 0, i), h_ctx, mod_ctx, g_ffn_pre, g_ffn_post, w_up[None], ffn_conv, w_down[None],
                     seq, t_ctx)
        h_lat = _ffn(f"ffn{i}_lat", (i, 0, i), h_lat, mod_lat, g_ffn_pre, g_ffn_post, w_up[None], ffn_conv, w_down[None],
                     dec_seq, t_lat_conv)

    return (h_ctx.reshape(batch, seq, d), h_lat.reshape(dec_batch, dec_seq, d),
            jnp.stack(new_k, axis=1), jnp.stack(new_v, axis=1))
```

```python
import functools

import jax
import jax.numpy as jnp
import numpy as np
from jax import lax
from jax.experimental import pallas as pl
from jax.experimental.pallas import tpu as pltpu

EPS = 1e-6
ROPE_THETA = 10000.0
GRID_W = 64
HEAD_DIM = 128
N_KV_HEADS = 2

V7X_SUBLANES = 8
V7X_BF16_ROWS = 16
V7X_VMEM_LIMIT_BYTES = 56 * 1024 * 1024

HALO = V7X_SUBLANES
CONV_CHUNK = 256
KEY_BLOCK = 1024
LOG2_E = 1.4426950408889634

BF16 = jnp.bfloat16
F32 = jnp.float32


def _resident(block_shape, index_map):
    return pl.BlockSpec(block_shape, index_map, pipeline_mode=pl.Buffered(1))


def _layer_spec(stacked, layer):
    zeros = (0,) * (stacked.ndim - 1)
    return _resident((None,) + stacked.shape[1:], lambda i: (layer,) + zeros)


def _with_side_casts(kernel_fn, n_in, n_out, n_casts):
    if n_casts == 0:
        return kernel_fn

    def body(*refs):
        ins, rest = refs[:n_in], refs[n_in:]
        srcs, rest = rest[:n_casts], rest[n_casts:]
        outs, rest = rest[:n_out], rest[n_out:]
        dsts, scratch = rest[:n_casts], rest[n_casts:]
        for src, dst in zip(srcs, dsts):
            dst[...] = src[...].astype(BF16)
        kernel_fn(*ins, *outs, *scratch)

    return body


def _cast_specs(casts, steps):
    in_specs, out_specs, out_shape = [], [], []
    for stacked, layer in casts:
        _, rows, cols = stacked.shape
        assert rows % (steps * V7X_BF16_ROWS) == 0
        blk = rows // steps
        in_specs.append(pl.BlockSpec((None, blk, cols), lambda i, layer=layer: (layer, i, 0)))
        out_specs.append(pl.BlockSpec((blk, cols), lambda i: (i, 0)))
        out_shape.append(jax.ShapeDtypeStruct((rows, cols), BF16))
    return in_specs, out_specs, out_shape


def _rms(x):
    return x * lax.rsqrt(jnp.mean(x * x, axis=-1, keepdims=True) + EPS)


def _dot(a, b):
    return jnp.dot(a, b, preferred_element_type=F32)


def _dot_nt(a, b):
    return lax.dot_general(a, b, (((1,), (1,)), ((), ())), preferred_element_type=F32)


def _mod_kernel(c_ref, w_ref, b_ref, o_ref):
    c = c_ref[...]
    s = c * (1.0 / (1.0 + jnp.exp(-c)))
    o_ref[0] = _dot(s.astype(BF16), w_ref[0].astype(BF16)) + b_ref[0]


def _modulation(cvec, mod_w, mod_b):
    depth, d, n = mod_w.shape
    rows = cvec.shape[0]
    tn = 1536
    return pl.pallas_call(
        _mod_kernel,
        grid=(depth, n // tn),
        in_specs=[
            pl.BlockSpec((rows, d), lambda l, j: (0, 0)),
            pl.BlockSpec((1, d, tn), lambda l, j: (l, 0, j)),
            pl.BlockSpec((1, 1, tn), lambda l, j: (l, 0, j)),
        ],
        out_specs=pl.BlockSpec((1, rows, tn), lambda l, j: (l, 0, j)),
        out_shape=jax.ShapeDtypeStruct((depth, rows, n), F32),
        compiler_params=pltpu.CompilerParams(dimension_semantics=("arbitrary", "arbitrary")),
        name="modulation",
    )(cvec, mod_w, mod_b.reshape(depth, 1, n))


def _fill_u(refs, has_halo, mod_ref, gpre_ref, ubuf, t, d, mod_off):
    shift = mod_ref[0, :, mod_off * d:(mod_off + 1) * d]
    scale = mod_ref[0, :, (mod_off + 1) * d:(mod_off + 2) * d]
    gs = gpre_ref[...] * (1.0 + scale)
    if has_halo:
        xp_ref, x_ref, xn_ref = refs
        halo = jnp.concatenate([xp_ref[...], xn_ref[...]], axis=0)
        ubuf[t:t + 2 * HALO, :] = (_rms(halo) * gs + shift).astype(BF16)
    else:
        (x_ref,) = refs
    ubuf[0:t, :] = (_rms(x_ref[...]) * gs + shift).astype(BF16)
    return x_ref


def _edge_flags(tiles_per_seq):
    i = pl.program_id(0)
    pos = lax.rem(i, tiles_per_seq)
    return (pos != 0).astype(F32), (pos != tiles_per_seq - 1).astype(F32)


def _row_masks(t, rows_per_seq):
    pos = lax.rem(lax.broadcasted_iota(jnp.int32, (t, CONV_CHUNK), 0), rows_per_seq)
    return pos == 0, pos == rows_per_seq - 1


def _conv3(h, k, t, has_halo, flags, first_row, last_row):
    zm = h[0:t]
    if has_halo:
        zp = h[t + HALO - 1:t + HALO] * flags[0]
        zn = h[t + HALO:t + HALO + 1] * flags[1]
    else:
        zp = jnp.zeros((1, h.shape[1]), F32)
        zn = zp
    z_prev = jnp.where(first_row, zp, pltpu.roll(zm, 1, 0))
    z_next = jnp.where(last_row, zn, pltpu.roll(zm, t - 1, 0))
    return z_prev * k[0:1] + zm * k[1:2] + z_next * k[2:3]


def _finish(x_ref, y, mod_ref, gpost_ref, o_ref, d, gate_off):
    gate = mod_ref[0, :, gate_off * d:(gate_off + 1) * d]
    o_ref[...] = x_ref[...] + (gate * gpost_ref[...]) * _rms(y)


def _mixer_kernel(*refs, t, d, has_halo, tiles_per_seq, rows_per_seq):
    n_x = 3 if has_halo else 1
    x_refs = refs[:n_x]
    mod_ref, gpre_ref, gpost_ref, win_ref, ck_ref, wout_ref, o_ref, ubuf, ybuf = refs[n_x:]
    x_ref = _fill_u(x_refs, has_halo, mod_ref, gpre_ref, ubuf, t, d, 0)
    flags = _edge_flags(tiles_per_seq) if has_halo else None
    first_row, last_row = _row_masks(t, rows_per_seq)
    for j in range(d // CONV_CHUNK):
        c0 = j * CONV_CHUNK
        b_gate = _dot(ubuf[0:t, :], win_ref[:, c0:c0 + CONV_CHUNK])
        c_gate = _dot(ubuf[...], win_ref[:, d + c0:d + c0 + CONV_CHUNK])
        xp = _dot(ubuf[...], win_ref[:, 2 * d + c0:2 * d + c0 + CONV_CHUNK])
        conv = _conv3(c_gate * xp, ck_ref[:, c0:c0 + CONV_CHUNK], t, has_halo, flags, first_row, last_row)
        ybuf[:, c0:c0 + CONV_CHUNK] = (b_gate * conv).astype(BF16)
    y = _dot(ybuf[...], wout_ref[...])
    _finish(x_ref, y, mod_ref, gpost_ref, o_ref, d, 2)


def _ffn_kernel(*refs, t, d, d_ff, has_halo, tiles_per_seq, rows_per_seq):
    n_x = 3 if has_halo else 1
    x_refs = refs[:n_x]
    mod_ref, gpre_ref, gpost_ref, wup_ref, ck_ref, wdown_ref, o_ref, ubuf, abuf = refs[n_x:]
    x_ref = _fill_u(x_refs, has_halo, mod_ref, gpre_ref, ubuf, t, d, 3)
    flags = _edge_flags(tiles_per_seq) if has_halo else None
    first_row, last_row = _row_masks(t, rows_per_seq)
    for j in range(d_ff // CONV_CHUNK):
        c0 = j * CONV_CHUNK
        hg = _dot(ubuf[...], wup_ref[:, c0:c0 + CONV_CHUNK])
        hu = _dot(ubuf[...], wup_ref[:, d_ff + c0:d_ff + c0 + CONV_CHUNK])
        g = _conv3(hg, ck_ref[:, c0:c0 + CONV_CHUNK], t, has_halo, flags, first_row, last_row)
        u = _conv3(hu, ck_ref[:, d_ff + c0:d_ff + c0 + CONV_CHUNK], t, has_halo, flags, first_row, last_row)
        act = g * (1.0 / (1.0 + jnp.exp(-g))) * u
        abuf[:, c0:c0 + CONV_CHUNK] = act.astype(BF16)
    y = _dot(abuf[...], wdown_ref[...])
    _finish(x_ref, y, mod_ref, gpost_ref, o_ref, d, 5)


def _conv_sublayer(kernel_fn, name, layer, x, mod, gpre, gpost, w_a, ck, w_b, seq_len, t, casts=()):
    rows, d = x.shape
    assert rows % t == 0 and t % V7X_BF16_ROWS == 0 and (seq_len % t == 0 or t % seq_len == 0)
    tiles_per_seq = max(seq_len // t, 1)
    has_halo = t < seq_len
    per_seq_mod = mod.shape[0] > 1
    assert not per_seq_mod or t <= seq_len
    hb = t // HALO
    last_hb = rows // HALO - 1

    def mod_map(i):
        return ((i // tiles_per_seq) if per_seq_mod else 0, 0, 0)

    x_specs = [pl.BlockSpec((t, d), lambda i: (i, 0))]
    x_args = [x]
    if has_halo:
        x_specs = [pl.BlockSpec((HALO, d), lambda i: (jnp.maximum(i * hb - 1, 0), 0)),
                   x_specs[0],
                   pl.BlockSpec((HALO, d), lambda i: (jnp.minimum((i + 1) * hb, last_hb), 0))]
        x_args = [x, x, x]
    m_rows = t + 2 * HALO if has_halo else t
    body = functools.partial(kernel_fn, t=t, d=d, has_halo=has_halo, tiles_per_seq=tiles_per_seq,
                             rows_per_seq=min(t, seq_len))
    in_specs = x_specs + [
        pl.BlockSpec((1, 1, mod.shape[2]), mod_map),
        _layer_spec(gpre, layer[0]), _layer_spec(gpost, layer[0]),
        _layer_spec(w_a, layer[1]), _layer_spec(ck, layer[2]), _layer_spec(w_b, layer[1]),
    ]
    cast_in, cast_out, cast_shape = _cast_specs(casts, rows // t)
    out = pl.pallas_call(
        _with_side_casts(body, len(in_specs), 1, len(casts)),
        grid=(rows // t,),
        in_specs=in_specs + cast_in,
        out_specs=[pl.BlockSpec((t, d), lambda i: (i, 0))] + cast_out,
        out_shape=[jax.ShapeDtypeStruct((rows, d), F32)] + cast_shape,
        scratch_shapes=[pltpu.VMEM((m_rows, d), BF16), pltpu.VMEM((t, w_b.shape[1]), BF16)],
        compiler_params=pltpu.CompilerParams(dimension_semantics=("arbitrary",),
                                             vmem_limit_bytes=V7X_VMEM_LIMIT_BYTES),
        name=name,
    )(*x_args, mod, gpre, gpost, w_a, ck, w_b, *[a for a, _ in casts])
    return out[0], out[1:]


def _mixer(name, layer, x, mod, gpre, gpost, w_in, ck, w_out, seq_len, t, casts=()):
    return _conv_sublayer(_mixer_kernel, name, layer, x, mod, gpre, gpost, w_in, ck, w_out, seq_len, t, casts)


def _ffn(name, layer, x, mod, gpre, gpost, w_up, ck, w_down, seq_len, t):
    kernel_fn = functools.partial(_ffn_kernel, d_ff=w_down.shape[1])
    return _conv_sublayer(kernel_fn, name, layer, x, mod, gpre, gpost, w_up, ck, w_down, seq_len, t)[0]


def _qkv_kernel(*refs, d, n_heads, rope, emit_f32_kv):
    x_ref, mod_ref, gpre_ref, w_ref, qg_ref, kg_ref = refs[:6]
    refs = refs[6:]
    if rope:
        cos_ref, sina_ref, sinb_ref = refs[:3]
        refs = refs[3:]
    q_ref, k_ref, v_ref = refs[:3]
    kv32_refs = refs[3:]
    t = x_ref.shape[0]
    shift = mod_ref[0, :, 0:d]
    scale = mod_ref[0, :, d:2 * d]
    u = (_rms(x_ref[...]) * (gpre_ref[...] * (1.0 + scale)) + shift).astype(BF16)
    qkv = _dot(u, w_ref[...])
    q_scale = HEAD_DIM ** -0.5 * LOG2_E

    ones = jnp.ones((HEAD_DIM, HEAD_DIM), BF16)

    def head(idx, gain):
        hv = qkv[:, idx * HEAD_DIM:(idx + 1) * HEAD_DIM]
        if rope:
            ms = _dot((hv * hv).astype(BF16), ones) * (1.0 / HEAD_DIM)
            hv = hv * lax.rsqrt(ms + EPS) * gain
        else:
            hv = _rms(hv) * gain
        if rope:
            hv = (hv * cos_ref[...]
                  + pltpu.roll(hv, HEAD_DIM - HEAD_DIM // 4, 1) * sina_ref[...]
                  + pltpu.roll(hv, HEAD_DIM // 4, 1) * sinb_ref[...])
        return hv

    for hq in range(n_heads):
        q_ref[:, hq * HEAD_DIM:(hq + 1) * HEAD_DIM] = (head(hq, qg_ref[...]) * q_scale).astype(BF16)
    for hk in range(N_KV_HEADS):
        lanes = slice(hk * HEAD_DIM, (hk + 1) * HEAD_DIM)
        kh = head(n_heads + hk, kg_ref[...])
        vh = qkv[:, (n_heads + N_KV_HEADS + hk) * HEAD_DIM:(n_heads + N_KV_HEADS + hk + 1) * HEAD_DIM]
        k_ref[:, lanes] = kh.astype(BF16)
        v_ref[:, lanes] = vh.astype(BF16)
        if emit_f32_kv:
            kv32_refs[0][pl.ds(hk, t, stride=N_KV_HEADS), :] = kh
            kv32_refs[1][pl.ds(hk, t, stride=N_KV_HEADS), :] = vh


def _qkv(name, layer, x, mod, gpre, w_qkv, q_gain, k_gain, seq_len, t, rope_tables=None, emit_f32_kv=False,
         casts=()):
    rows, d = x.shape
    n_heads = d // HEAD_DIM
    kv_w = N_KV_HEADS * HEAD_DIM
    tiles_per_seq = max(seq_len // t, 1)
    per_seq_mod = mod.shape[0] > 1
    rope = rope_tables is not None
    assert not (per_seq_mod or rope) or seq_len % t == 0

    def mod_map(i):
        return ((i // tiles_per_seq) if per_seq_mod else 0, 0, 0)

    in_specs = [
        pl.BlockSpec((t, d), lambda i: (i, 0)),
        pl.BlockSpec((1, 1, mod.shape[2]), mod_map),
        _layer_spec(gpre, layer[0]), _layer_spec(w_qkv, layer[1]),
        _layer_spec(q_gain, layer[1]), _layer_spec(k_gain, layer[1]),
    ]
    args = [x, mod, gpre, w_qkv, q_gain, k_gain]
    if rope:
        in_specs += [pl.BlockSpec((t, HEAD_DIM), lambda i: (i % tiles_per_seq, 0))] * 3
        args += list(rope_tables)
    out_specs = [pl.BlockSpec((t, d), lambda i: (i, 0)),
                 pl.BlockSpec((t, kv_w), lambda i: (i, 0)),
                 pl.BlockSpec((t, kv_w), lambda i: (i, 0))]
    out_shape = [jax.ShapeDtypeStruct((rows, d), BF16),
                 jax.ShapeDtypeStruct((rows, kv_w), BF16),
                 jax.ShapeDtypeStruct((rows, kv_w), BF16)]
    if emit_f32_kv:
        out_specs += [pl.BlockSpec((N_KV_HEADS * t, HEAD_DIM), lambda i: (i, 0))] * 2
        out_shape += [jax.ShapeDtypeStruct((N_KV_HEADS * rows, HEAD_DIM), F32)] * 2
    cast_in, cast_out, cast_shape = _cast_specs(casts, rows // t)
    body = functools.partial(_qkv_kernel, d=d, n_heads=n_heads, rope=rope, emit_f32_kv=emit_f32_kv)
    out = pl.pallas_call(
        _with_side_casts(body, len(in_specs), len(out_specs), len(casts)),
        grid=(rows // t,),
        in_specs=in_specs + cast_in,
        out_specs=out_specs + cast_out,
        out_shape=out_shape + cast_shape,
        compiler_params=pltpu.CompilerParams(dimension_semantics=("arbitrary",),
                                             vmem_limit_bytes=V7X_VMEM_LIMIT_BYTES),
        name=name,
    )(*args, *[a for a, _ in casts])
    return out[:len(out_specs)], out[len(out_specs):]


def _kv_block(ref, kk, b0, kb):
    if ref.shape[1] == HEAD_DIM:
        return ref[pl.ds(N_KV_HEADS * b0 + kk, kb, stride=N_KV_HEADS), :].astype(BF16)
    return ref[b0:b0 + kb, kk * HEAD_DIM:(kk + 1) * HEAD_DIM]


def _attn_kernel(*refs, d, n_heads, n_kv_sets, seqs_per_tile):
    x_ref, q_ref = refs[:2]
    kv_refs = refs[2:2 + 2 * n_kv_sets]
    mod_ref, gpost_ref, wo_ref, o_ref, obuf = refs[2 + 2 * n_kv_sets:]
    tq = q_ref.shape[0] // seqs_per_tile
    group = n_heads // N_KV_HEADS
    for sub, kk in [(a, b) for a in range(seqs_per_tile) for b in range(N_KV_HEADS)]:
        rows = slice(sub * tq, (sub + 1) * tq)
        qs = jnp.concatenate(
            [q_ref[rows, (kk * group + g) * HEAD_DIM:(kk * group + g + 1) * HEAD_DIM] for g in range(group)], axis=0)
        m = denom = acc = None
        for s in range(n_kv_sets):
            k_ref, v_ref = kv_refs[2 * s], kv_refs[2 * s + 1]
            n_keys = k_ref.shape[0] * k_ref.shape[1] // (N_KV_HEADS * HEAD_DIM * seqs_per_tile)
            kb = min(KEY_BLOCK, n_keys)
            for b0 in range(sub * n_keys, (sub + 1) * n_keys, kb):
                sc = _dot_nt(qs, _kv_block(k_ref, kk, b0, kb))
                v_blk = _kv_block(v_ref, kk, b0, kb)
                mb = sc.max(axis=-1, keepdims=True)
                if m is None:
                    m = mb
                    p = jnp.exp2(sc - m)
                    denom = p.sum(axis=-1, keepdims=True)
                    acc = _dot(p.astype(BF16), v_blk)
                else:
                    m_new = jnp.maximum(m, mb)
                    alpha = jnp.exp2(m - m_new)
                    p = jnp.exp2(sc - m_new)
                    denom = alpha * denom + p.sum(axis=-1, keepdims=True)
                    acc = alpha * acc + _dot(p.astype(BF16), v_blk)
                    m = m_new
        out = acc * (1.0 / denom)
        for g in range(group):
            hq = kk * group + g
            obuf[rows, hq * HEAD_DIM:(hq + 1) * HEAD_DIM] = out[g * tq:(g + 1) * tq].astype(BF16)
    y = _dot(obuf[...], wo_ref[...])
    _finish(x_ref, y, mod_ref, gpost_ref, o_ref, d, 2)


def _attention(name, layer, x, q, kv_sets, mod, gpost, w_o, seq_len, tq, casts=()):
    rows, d = x.shape
    n_heads = d // HEAD_DIM
    kv_w = N_KV_HEADS * HEAD_DIM
    assert seq_len % tq == 0 or tq % seq_len == 0
    tiles_per_seq = max(seq_len // tq, 1)
    seqs_per_tile = max(tq // seq_len, 1)
    per_seq_mod = mod.shape[0] > 1
    assert not per_seq_mod or seqs_per_tile == 1

    def mod_map(i):
        return ((i // tiles_per_seq) if per_seq_mod else 0, 0, 0)

    in_specs = [pl.BlockSpec((tq, d), lambda i: (i, 0)), pl.BlockSpec((tq, d), lambda i: (i, 0))]
    args = [x, q]
    for k, v, n in kv_sets:
        block = (seqs_per_tile * n * kv_w // k.shape[1], k.shape[1])
        in_specs += [pl.BlockSpec(block, lambda i: (i // tiles_per_seq, 0))] * 2
        args += [k, v]
    in_specs += [pl.BlockSpec((1, 1, mod.shape[2]), mod_map),
                 _layer_spec(gpost, layer[0]), _layer_spec(w_o, layer[1])]
    args += [mod, gpost, w_o]
    cast_in, cast_out, cast_shape = _cast_specs(casts, rows // tq)
    body = functools.partial(_attn_kernel, d=d, n_heads=n_heads, n_kv_sets=len(kv_sets), seqs_per_tile=seqs_per_tile)
    out = pl.pallas_call(
        _with_side_casts(body, len(in_specs), 1, len(casts)),
        grid=(rows // tq,),
        in_specs=in_specs + cast_in,
        out_specs=[pl.BlockSpec((tq, d), lambda i: (i, 0))] + cast_out,
        out_shape=[jax.ShapeDtypeStruct((rows, d), F32)] + cast_shape,
        scratch_shapes=[pltpu.VMEM((tq, d), BF16)],
        compiler_params=pltpu.CompilerParams(dimension_semantics=("arbitrary",),
                                             vmem_limit_bytes=V7X_VMEM_LIMIT_BYTES),
        name=name,
    )(*args, *[a for a, _ in casts])
    return out[0], out[1:]


def _rope_tables(n_lat):
    half = HEAD_DIM // 2
    rows_n = n_lat // GRID_W
    row = np.repeat(np.arange(rows_n), GRID_W).astype(np.float32)
    col = np.tile(np.arange(GRID_W), rows_n).astype(np.float32)
    inv = (np.float32(ROPE_THETA) ** (-np.arange(0, half, 2, dtype=np.float32) / np.float32(half))).astype(np.float32)
    ang_r = row[:, None] * inv[None, :]
    ang_c = col[:, None] * inv[None, :]
    cr, sr, cc, sc = np.cos(ang_r), np.sin(ang_r), np.cos(ang_c), np.sin(ang_c)
    zero = np.zeros_like(sr)
    cos = np.concatenate([cr, cr, cc, cc], axis=-1)
    sin_a = np.concatenate([-sr, zero, -sc, zero], axis=-1)
    sin_b = np.concatenate([zero, sr, zero, sc], axis=-1)
    return jnp.asarray(cos), jnp.asarray(sin_a), jnp.asarray(sin_b)


def kernel(x_prompt, x_sample, cache_k, cache_v, c, c_ctx, mod_w, mod_b, norm_mix_pre, norm_mix_post, norm_ffn_pre, norm_ffn_post, conv_w_in, conv_k, conv_w_out, attn_w_qkv, attn_q_gain, attn_k_gain, attn_w_o, ffn_w_up, ffn_conv, ffn_w_down):
    batch, seq, d = x_prompt.shape
    dec_batch, dec_seq, _ = x_sample.shape
    depth = mod_w.shape[0]
    past_len = cache_k.shape[2]
    t_ctx = 2 * seq
    tq_ctx = 2 * seq
    t_lat_conv = 1024
    t_lat_qkv = 512
    tq_lat = 256

    n_cond = dec_batch + 1
    pad = (-n_cond) % V7X_BF16_ROWS
    cvec = jnp.concatenate([c, c_ctx[None, :], jnp.zeros((pad, d), F32)], axis=0)
    mod_all = _modulation(cvec, mod_w, mod_b)

    per_layer_rows = lambda a: a.reshape(a.shape[0], 1, a.shape[1])
    g_mix_pre, g_mix_post = per_layer_rows(norm_mix_pre), per_layer_rows(norm_mix_post)
    g_ffn_pre, g_ffn_post = per_layer_rows(norm_ffn_pre), per_layer_rows(norm_ffn_post)
    q_gain, k_gain = per_layer_rows(attn_q_gain), per_layer_rows(attn_k_gain)
    w_in, w_out = conv_w_in.astype(BF16), conv_w_out.astype(BF16)
    w_qkv, w_o = attn_w_qkv.astype(BF16), attn_w_o.astype(BF16)

    h_ctx = x_prompt.reshape(batch * seq, d)
    h_lat = x_sample.reshape(dec_batch * dec_seq, d)
    rope_tables = _rope_tables(dec_seq)
    new_k, new_v = [], []

    for i in range(depth):
        j = i // 2
        mod_lat = mod_all[i, :dec_batch][:, None, :]
        mod_ctx = mod_all[i, dec_batch:dec_batch + 1][:, None, :]
        if i % 2 == 0:
            h_ctx, (w_down,) = _mixer(f"mixer{i}_ctx", (i, j, j), h_ctx, mod_ctx, g_mix_pre, g_mix_post, w_in, conv_k,
                                      w_out, seq, t_ctx, casts=[(ffn_w_down, i)])
            h_lat, (w_up,) = _mixer(f"mixer{i}_lat", (i, j, j), h_lat, mod_lat, g_mix_pre, g_mix_post, w_in, conv_k,
                                    w_out, dec_seq, t_lat_conv, casts=[(ffn_w_up, i)])
        else:
            (q_c, k_c, v_c, k_c32, v_c32), (w_down,) = _qkv(
                f"qkv{i}_ctx", (i, j), h_ctx, mod_ctx, g_mix_pre, w_qkv, q_gain, k_gain, seq, t_ctx,
                emit_f32_kv=True, casts=[(ffn_w_down, i)])
            new_k.append(k_c32.reshape(batch, seq, N_KV_HEADS, HEAD_DIM))
            new_v.append(v_c32.reshape(batch, seq, N_KV_HEADS, HEAD_DIM))
            h_ctx, (w_up,) = _attention(f"attn{i}_ctx", (i, j), h_ctx, q_c, [(k_c, v_c, seq)], mod_ctx, g_mix_post, w_o,
                                        seq, tq_ctx, casts=[(ffn_w_up, i)])
            (q_l, k_l, v_l), _ = _qkv(f"qkv{i}_lat", (i, j), h_lat, mod_lat, g_mix_pre, w_qkv, q_gain, k_gain,
                                      dec_seq, t_lat_qkv, rope_tables=rope_tables)
            ck = cache_k[:, j].reshape(dec_batch * past_len * N_KV_HEADS, HEAD_DIM)
            cv = cache_v[:, j].reshape(dec_batch * past_len * N_KV_HEADS, HEAD_DIM)
            h_lat, _ = _attention(f"attn{i}_lat", (i, j), h_lat, q_l, [(ck, cv, past_len), (k_l, v_l, dec_seq)],
                                  mod_lat, g_mix_post, w_o, dec_seq, tq_lat)
        h_ctx = _ffn(f"ffn{i}_ctx", (i, 0, i), h_ctx, mod_ctx, g_ffn_pre, g_ffn_post, w_up[None], ffn_conv, w_down[None],
                     seq, t_ctx)
        h_lat = _ffn(f"ffn{i}_lat", (i, 0, i), h_lat, mod_lat, g_ffn_pre, g_ffn_post, w_up[None], ffn_conv, w_down[None],
                     dec_seq, t_lat_conv)

    return (h_ctx.reshape(batch, seq, d), h_lat.reshape(dec_batch, dec_seq, d),
            jnp.stack(new_k, axis=1), jnp.stack(new_v, axis=1))
```

```python
import functools

import jax
import jax.numpy as jnp
import numpy as np
from jax import lax
from jax.experimental import pallas as pl
from jax.experimental.pallas import tpu as pltpu

EPS = 1e-6
ROPE_THETA = 10000.0
GRID_W = 64
HEAD_DIM = 128
N_KV_HEADS = 2

V7X_SUBLANES = 8
V7X_BF16_ROWS = 16
V7X_VMEM_LIMIT_BYTES = 56 * 1024 * 1024

HALO = V7X_SUBLANES
CONV_CHUNK = 256
KEY_BLOCK = 1024
LOG2_E = 1.4426950408889634

BF16 = jnp.bfloat16
F32 = jnp.float32


def _resident(block_shape, index_map):
    return pl.BlockSpec(block_shape, index_map, pipeline_mode=pl.Buffered(1))


def _layer_spec(stacked, layer):
    zeros = (0,) * (stacked.ndim - 1)
    return _resident((None,) + stacked.shape[1:], lambda i: (layer,) + zeros)


def _with_side_casts(kernel_fn, n_in, n_out, n_casts):
    if n_casts == 0:
        return kernel_fn

    def body(*refs):
        ins, rest = refs[:n_in], refs[n_in:]
        srcs, rest = rest[:n_casts], rest[n_casts:]
        outs, rest = rest[:n_out], rest[n_out:]
        dsts, scratch = rest[:n_casts], rest[n_casts:]
        for src, dst in zip(srcs, dsts):
            dst[...] = src[...].astype(BF16)
        kernel_fn(*ins, *outs, *scratch)

    return body


def _cast_specs(casts, steps):
    in_specs, out_specs, out_shape = [], [], []
    for stacked, layer in casts:
        _, rows, cols = stacked.shape
        assert rows % (steps * V7X_BF16_ROWS) == 0
        blk = rows // steps
        in_specs.append(pl.BlockSpec((None, blk, cols), lambda i, layer=layer: (layer, i, 0)))
        out_specs.append(pl.BlockSpec((blk, cols), lambda i: (i, 0)))
        out_shape.append(jax.ShapeDtypeStruct((rows, cols), BF16))
    return in_specs, out_specs, out_shape


def _rms(x):
    return x * lax.rsqrt(jnp.mean(x * x, axis=-1, keepdims=True) + EPS)


def _dot(a, b):
    return jnp.dot(a, b, preferred_element_type=F32)


def _dot_nt(a, b):
    return lax.dot_general(a, b, (((1,), (1,)), ((), ())), preferred_element_type=F32)


def _mod_kernel(c_ref, w_ref, b_ref, o_ref):
    c = c_ref[...]
    s = c * (1.0 / (1.0 + jnp.exp(-c)))
    o_ref[0] = _dot(s.astype(BF16), w_ref[0].astype(BF16)) + b_ref[0]


def _modulation(cvec, mod_w, mod_b):
    depth, d, n = mod_w.shape
    rows = cvec.shape[0]
    tn = 1536
    return pl.pallas_call(
        _mod_kernel,
        grid=(depth, n // tn),
        in_specs=[
            pl.BlockSpec((rows, d), lambda l, j: (0, 0)),
            pl.BlockSpec((1, d, tn), lambda l, j: (l, 0, j)),
            pl.BlockSpec((1, 1, tn), lambda l, j: (l, 0, j)),
        ],
        out_specs=pl.BlockSpec((1, rows, tn), lambda l, j: (l, 0, j)),
        out_shape=jax.ShapeDtypeStruct((depth, rows, n), F32),
        compiler_params=pltpu.CompilerParams(dimension_semantics=("arbitrary", "arbitrary")),
        name="modulation",
    )(cvec, mod_w, mod_b.reshape(depth, 1, n))


def _fill_u(refs, has_halo, mod_ref, gpre_ref, ubuf, t, d, mod_off):
    shift = mod_ref[0, :, mod_off * d:(mod_off + 1) * d]
    scale = mod_ref[0, :, (mod_off + 1) * d:(mod_off + 2) * d]
    gs = gpre_ref[...] * (1.0 + scale)
    if has_halo:
        xp_ref, x_ref, xn_ref = refs
        halo = jnp.concatenate([xp_ref[...], xn_ref[...]], axis=0)
        ubuf[t:t + 2 * HALO, :] = (_rms(halo) * gs + shift).astype(BF16)
    else:
        (x_ref,) = refs
    ubuf[0:t, :] = (_rms(x_ref[...]) * gs + shift).astype(BF16)
    return x_ref


def _edge_flags(tiles_per_seq):
    i = pl.program_id(0)
    pos = lax.rem(i, tiles_per_seq)
    return (pos != 0).astype(F32), (pos != tiles_per_seq - 1).astype(F32)


def _row_masks(t, rows_per_seq):
    pos = lax.rem(lax.broadcasted_iota(jnp.int32, (t, CONV_CHUNK), 0), rows_per_seq)
    return pos == 0, pos == rows_per_seq - 1


def _conv3(h, k, t, has_halo, flags, first_row, last_row):
    zm = h[0:t]
    if has_halo:
        zp = h[t + HALO - 1:t + HALO] * flags[0]
        zn = h[t + HALO:t + HALO + 1] * flags[1]
    else:
        zp = jnp.zeros((1, h.shape[1]), F32)
        zn = zp
    z_prev = jnp.where(first_row, zp, pltpu.roll(zm, 1, 0))
    z_next = jnp.where(last_row, zn, pltpu.roll(zm, t - 1, 0))
    return z_prev * k[0:1] + zm * k[1:2] + z_next * k[2:3]


def _finish(x_ref, y, mod_ref, gpost_ref, o_ref, d, gate_off):
    gate = mod_ref[0, :, gate_off * d:(gate_off + 1) * d]
    o_ref[...] = x_ref[...] + (gate * gpost_ref[...]) * _rms(y)


def _mixer_kernel(*refs, t, d, has_halo, tiles_per_seq, rows_per_seq):
    n_x = 3 if has_halo else 1
    x_refs = refs[:n_x]
    mod_ref, gpre_ref, gpost_ref, win_ref, ck_ref, wout_ref, o_ref, ubuf, ybuf = refs[n_x:]
    x_ref = _fill_u(x_refs, has_halo, mod_ref, gpre_ref, ubuf, t, d, 0)
    flags = _edge_flags(tiles_per_seq) if has_halo else None
    first_row, last_row = _row_masks(t, rows_per_seq)
    for j in range(d // CONV_CHUNK):
        c0 = j * CONV_CHUNK
        b_gate = _dot(ubuf[0:t, :], win_ref[:, c0:c0 + CONV_CHUNK])
        c_gate = _dot(ubuf[...], win_ref[:, d + c0:d + c0 + CONV_CHUNK])
        xp = _dot(ubuf[...], win_ref[:, 2 * d + c0:2 * d + c0 + CONV_CHUNK])
        conv = _conv3(c_gate * xp, ck_ref[:, c0:c0 + CONV_CHUNK], t, has_halo, flags, first_row, last_row)
        ybuf[:, c0:c0 + CONV_CHUNK] = (b_gate * conv).astype(BF16)
    y = _dot(ybuf[...], wout_ref[...])
    _finish(x_ref, y, mod_ref, gpost_ref, o_ref, d, 2)


def _ffn_kernel(*refs, t, d, d_ff, has_halo, tiles_per_seq, rows_per_seq):
    n_x = 3 if has_halo else 1
    x_refs = refs[:n_x]
    mod_ref, gpre_ref, gpost_ref, wup_ref, ck_ref, wdown_ref, o_ref, ubuf, abuf = refs[n_x:]
    x_ref = _fill_u(x_refs, has_halo, mod_ref, gpre_ref, ubuf, t, d, 3)
    flags = _edge_flags(tiles_per_seq) if has_halo else None
    first_row, last_row = _row_masks(t, rows_per_seq)
    for j in range(d_ff // CONV_CHUNK):
        c0 = j * CONV_CHUNK
        hg = _dot(ubuf[...], wup_ref[:, c0:c0 + CONV_CHUNK])
        hu = _dot(ubuf[...], wup_ref[:, d_ff + c0:d_ff + c0 + CONV_CHUNK])
        g = _conv3(hg, ck_ref[:, c0:c0 + CONV_CHUNK], t, has_halo, flags, first_row, last_row)
        u = _conv3(hu, ck_ref[:, d_ff + c0:d_ff + c0 + CONV_CHUNK], t, has_halo, flags, first_row, last_row)
        act = g * (1.0 / (1.0 + jnp.exp(-g))) * u
        abuf[:, c0:c0 + CONV_CHUNK] = act.astype(BF16)
    y = _dot(abuf[...], wdown_ref[...])
    _finish(x_ref, y, mod_ref, gpost_ref, o_ref, d, 5)


def _conv_sublayer(kernel_fn, name, layer, x, mod, gpre, gpost, w_a, ck, w_b, seq_len, t, casts=()):
    rows, d = x.shape
    assert rows % t == 0 and t % V7X_BF16_ROWS == 0 and (seq_len % t == 0 or t % seq_len == 0)
    tiles_per_seq = max(seq_len // t, 1)
    has_halo = t < seq_len
    per_seq_mod = mod.shape[0] > 1
    assert not per_seq_mod or t <= seq_len
    hb = t // HALO
    last_hb = rows // HALO - 1

    def mod_map(i):
        return ((i // tiles_per_seq) if per_seq_mod else 0, 0, 0)

    x_specs = [pl.BlockSpec((t, d), lambda i: (i, 0))]
    x_args = [x]
    if has_halo:
        x_specs = [pl.BlockSpec((HALO, d), lambda i: (jnp.maximum(i * hb - 1, 0), 0)),
                   x_specs[0],
                   pl.BlockSpec((HALO, d), lambda i: (jnp.minimum((i + 1) * hb, last_hb), 0))]
        x_args = [x, x, x]
    m_rows = t + 2 * HALO if has_halo else t
    body = functools.partial(kernel_fn, t=t, d=d, has_halo=has_halo, tiles_per_seq=tiles_per_seq,
                             rows_per_seq=min(t, seq_len))
    in_specs = x_specs + [
        pl.BlockSpec((1, 1, mod.shape[2]), mod_map),
        _layer_spec(gpre, layer[0]), _layer_spec(gpost, layer[0]),
        _layer_spec(w_a, layer[1]), _layer_spec(ck, layer[2]), _layer_spec(w_b, layer[1]),
    ]
    cast_in, cast_out, cast_shape = _cast_specs(casts, rows // t)
    out = pl.pallas_call(
        _with_side_casts(body, len(in_specs), 1, len(casts)),
        grid=(rows // t,),
        in_specs=in_specs + cast_in,
        out_specs=[pl.BlockSpec((t, d), lambda i: (i, 0))] + cast_out,
        out_shape=[jax.ShapeDtypeStruct((rows, d), F32)] + cast_shape,
        scratch_shapes=[pltpu.VMEM((m_rows, d), BF16), pltpu.VMEM((t, w_b.shape[1]), BF16)],
        compiler_params=pltpu.CompilerParams(dimension_semantics=("arbitrary",),
                                             vmem_limit_bytes=V7X_VMEM_LIMIT_BYTES),
        name=name,
    )(*x_args, mod, gpre, gpost, w_a, ck, w_b, *[a for a, _ in casts])
    return out[0], out[1:]


def _mixer(name, layer, x, mod, gpre, gpost, w_in, ck, w_out, seq_len, t, casts=()):
    return _conv_sublayer(_mixer_kernel, name, layer, x, mod, gpre, gpost, w_in, ck, w_out, seq_len, t, casts)


def _ffn(name, layer, x, mod, gpre, gpost, w_up, ck, w_down, seq_len, t):
    kernel_fn = functools.partial(_ffn_kernel, d_ff=w_down.shape[1])
    return _conv_sublayer(kernel_fn, name, layer, x, mod, gpre, gpost, w_up, ck, w_down, seq_len, t)[0]


def _qkv_kernel(*refs, d, n_heads, rope, emit_f32_kv):
    x_ref, mod_ref, gpre_ref, w_ref, qg_ref, kg_ref = refs[:6]
    refs = refs[6:]
    if rope:
        cos_ref, sina_ref, sinb_ref = refs[:3]
        refs = refs[3:]
    q_ref, k_ref, v_ref = refs[:3]
    n_kv32 = 2 if emit_f32_kv else 0
    kv32_refs = refs[3:3 + n_kv32]
    t = x_ref.shape[0]
    shift = mod_ref[0, :, 0:d]
    scale = mod_ref[0, :, d:2 * d]
    u = (_rms(x_ref[...]) * (gpre_ref[...] * (1.0 + scale)) + shift).astype(BF16)
    q_scale = HEAD_DIM ** -0.5 * LOG2_E
    pair = 2 * HEAD_DIM
    n_pairs = (n_heads + N_KV_HEADS) // 2

    if rope:
        (ubuf,) = refs[3 + n_kv32:]
        ubuf[...] = u
        project = lambda p: _dot(ubuf[...], w_ref[:, p * pair:(p + 1) * pair])
        in_flight = 2
    else:
        qkv = _dot(u, w_ref[...])
        project = lambda p: qkv[:, p * pair:(p + 1) * pair]
        in_flight = n_pairs + 1

    def issue_after(done):
        tile = (slice(0, V7X_BF16_ROWS), slice(0, HEAD_DIM))
        blk = done[0:V7X_BF16_ROWS, :]
        ubuf[tile] = ubuf[tile] + (blk - blk).astype(BF16)

    ones = jnp.ones((HEAD_DIM, HEAD_DIM), BF16)

    def head(hv, gain):
        if rope:
            ms = _dot((hv * hv).astype(BF16), ones) * (1.0 / HEAD_DIM)
            hv = hv * lax.rsqrt(ms + EPS) * gain
            hv = (hv * cos_ref[...]
                  + pltpu.roll(hv, HEAD_DIM - HEAD_DIM // 4, 1) * sina_ref[...]
                  + pltpu.roll(hv, HEAD_DIM // 4, 1) * sinb_ref[...])
        else:
            hv = _rms(hv) * gain
        return hv

    raw = {p: project(p) for p in range(min(in_flight, n_pairs + 1))}
    for p in range(n_pairs):
        first = None
        for idx in (2 * p, 2 * p + 1):
            lanes = slice((idx % 2) * HEAD_DIM, (idx % 2 + 1) * HEAD_DIM)
            if idx < n_heads:
                hv = head(raw[p][:, lanes], qg_ref[...])
                q_ref[:, idx * HEAD_DIM:(idx + 1) * HEAD_DIM] = (hv * q_scale).astype(BF16)
            else:
                hk = idx - n_heads
                hv = head(raw[p][:, lanes], kg_ref[...])
                k_ref[:, hk * HEAD_DIM:(hk + 1) * HEAD_DIM] = hv.astype(BF16)
                if emit_f32_kv:
                    kv32_refs[0][pl.ds(hk, t, stride=N_KV_HEADS), :] = hv
            first = hv if first is None else first
        del raw[p]
        if p + in_flight <= n_pairs:
            issue_after(first)
            raw[p + in_flight] = project(p + in_flight)
    v = raw[n_pairs]
    for hk in range(N_KV_HEADS):
        lanes = slice(hk * HEAD_DIM, (hk + 1) * HEAD_DIM)
        v_ref[:, lanes] = v[:, lanes].astype(BF16)
        if emit_f32_kv:
            kv32_refs[1][pl.ds(hk, t, stride=N_KV_HEADS), :] = v[:, lanes]


def _qkv(name, layer, x, mod, gpre, w_qkv, q_gain, k_gain, seq_len, t, rope_tables=None, emit_f32_kv=False,
         casts=()):
    rows, d = x.shape
    n_heads = d // HEAD_DIM
    kv_w = N_KV_HEADS * HEAD_DIM
    tiles_per_seq = max(seq_len // t, 1)
    per_seq_mod = mod.shape[0] > 1
    rope = rope_tables is not None
    assert not (per_seq_mod or rope) or seq_len % t == 0

    def mod_map(i):
        return ((i // tiles_per_seq) if per_seq_mod else 0, 0, 0)

    in_specs = [
        pl.BlockSpec((t, d), lambda i: (i, 0)),
        pl.BlockSpec((1, 1, mod.shape[2]), mod_map),
        _layer_spec(gpre, layer[0]), _layer_spec(w_qkv, layer[1]),
        _layer_spec(q_gain, layer[1]), _layer_spec(k_gain, layer[1]),
    ]
    args = [x, mod, gpre, w_qkv, q_gain, k_gain]
    if rope:
        in_specs += [pl.BlockSpec((t, HEAD_DIM), lambda i: (i % tiles_per_seq, 0))] * 3
        args += list(rope_tables)
    out_specs = [pl.BlockSpec((t, d), lambda i: (i, 0)),
                 pl.BlockSpec((t, kv_w), lambda i: (i, 0)),
                 pl.BlockSpec((t, kv_w), lambda i: (i, 0))]
    out_shape = [jax.ShapeDtypeStruct((rows, d), BF16),
                 jax.ShapeDtypeStruct((rows, kv_w), BF16),
                 jax.ShapeDtypeStruct((rows, kv_w), BF16)]
    if emit_f32_kv:
        out_specs += [pl.BlockSpec((N_KV_HEADS * t, HEAD_DIM), lambda i: (i, 0))] * 2
        out_shape += [jax.ShapeDtypeStruct((N_KV_HEADS * rows, HEAD_DIM), F32)] * 2
    cast_in, cast_out, cast_shape = _cast_specs(casts, rows // t)
    body = functools.partial(_qkv_kernel, d=d, n_heads=n_heads, rope=rope, emit_f32_kv=emit_f32_kv)
    out = pl.pallas_call(
        _with_side_casts(body, len(in_specs), len(out_specs), len(casts)),
        grid=(rows // t,),
        in_specs=in_specs + cast_in,
        out_specs=out_specs + cast_out,
        out_shape=out_shape + cast_shape,
        scratch_shapes=[pltpu.VMEM((t, d), BF16)] if rope else [],
        compiler_params=pltpu.CompilerParams(dimension_semantics=("arbitrary",),
                                             vmem_limit_bytes=V7X_VMEM_LIMIT_BYTES),
        name=name,
    )(*args, *[a for a, _ in casts])
    return out[:len(out_specs)], out[len(out_specs):]


def _kv_block(ref, kk, b0, kb):
    if ref.shape[1] == HEAD_DIM:
        return ref[pl.ds(N_KV_HEADS * b0 + kk, kb, stride=N_KV_HEADS), :].astype(BF16)
    return ref[b0:b0 + kb, kk * HEAD_DIM:(kk + 1) * HEAD_DIM]


def _attn_kernel(*refs, d, n_heads, n_kv_sets, seqs_per_tile):
    x_ref, q_ref = refs[:2]
    kv_refs = refs[2:2 + 2 * n_kv_sets]
    mod_ref, gpost_ref, wo_ref, o_ref, obuf = refs[2 + 2 * n_kv_sets:]
    tq = q_ref.shape[0] // seqs_per_tile
    group = n_heads // N_KV_HEADS
    for sub, kk in [(a, b) for a in range(seqs_per_tile) for b in range(N_KV_HEADS)]:
        rows = slice(sub * tq, (sub + 1) * tq)
        qs = jnp.concatenate(
            [q_ref[rows, (kk * group + g) * HEAD_DIM:(kk * group + g + 1) * HEAD_DIM] for g in range(group)], axis=0)
        m = denom = acc = None
        for s in range(n_kv_sets):
            k_ref, v_ref = kv_refs[2 * s], kv_refs[2 * s + 1]
            n_keys = k_ref.shape[0] * k_ref.shape[1] // (N_KV_HEADS * HEAD_DIM * seqs_per_tile)
            kb = min(KEY_BLOCK, n_keys)
            for b0 in range(sub * n_keys, (sub + 1) * n_keys, kb):
                sc = _dot_nt(qs, _kv_block(k_ref, kk, b0, kb))
                v_blk = _kv_block(v_ref, kk, b0, kb)
                mb = sc.max(axis=-1, keepdims=True)
                if m is None:
                    m = mb
                    p = jnp.exp2(sc - m)
                    denom = p.sum(axis=-1, keepdims=True)
                    acc = _dot(p.astype(BF16), v_blk)
                else:
                    m_new = jnp.maximum(m, mb)
                    alpha = jnp.exp2(m - m_new)
                    p = jnp.exp2(sc - m_new)
                    denom = alpha * denom + p.sum(axis=-1, keepdims=True)
                    acc = alpha * acc + _dot(p.astype(BF16), v_blk)
                    m = m_new
        out = acc * (1.0 / denom)
        for g in range(group):
            hq = kk * group + g
            obuf[rows, hq * HEAD_DIM:(hq + 1) * HEAD_DIM] = out[g * tq:(g + 1) * tq].astype(BF16)
    y = _dot(obuf[...], wo_ref[...])
    _finish(x_ref, y, mod_ref, gpost_ref, o_ref, d, 2)


def _attention(name, layer, x, q, kv_sets, mod, gpost, w_o, seq_len, tq, casts=()):
    rows, d = x.shape
    n_heads = d // HEAD_DIM
    kv_w = N_KV_HEADS * HEAD_DIM
    assert seq_len % tq == 0 or tq % seq_len == 0
    tiles_per_seq = max(seq_len // tq, 1)
    seqs_per_tile = max(tq // seq_len, 1)
    per_seq_mod = mod.shape[0] > 1
    assert not per_seq_mod or seqs_per_tile == 1

    def mod_map(i):
        return ((i // tiles_per_seq) if per_seq_mod else 0, 0, 0)

    in_specs = [pl.BlockSpec((tq, d), lambda i: (i, 0)), pl.BlockSpec((tq, d), lambda i: (i, 0))]
    args = [x, q]
    for k, v, n in kv_sets:
        block = (seqs_per_tile * n * kv_w // k.shape[1], k.shape[1])
        in_specs += [pl.BlockSpec(block, lambda i: (i // tiles_per_seq, 0))] * 2
        args += [k, v]
    in_specs += [pl.BlockSpec((1, 1, mod.shape[2]), mod_map),
                 _layer_spec(gpost, layer[0]), _layer_spec(w_o, layer[1])]
    args += [mod, gpost, w_o]
    cast_in, cast_out, cast_shape = _cast_specs(casts, rows // tq)
    body = functools.partial(_attn_kernel, d=d, n_heads=n_heads, n_kv_sets=len(kv_sets), seqs_per_tile=seqs_per_tile)
    out = pl.pallas_call(
        _with_side_casts(body, len(in_specs), 1, len(casts)),
        grid=(rows // tq,),
        in_specs=in_specs + cast_in,
        out_specs=[pl.BlockSpec((tq, d), lambda i: (i, 0))] + cast_out,
        out_shape=[jax.ShapeDtypeStruct((rows, d), F32)] + cast_shape,
        scratch_shapes=[pltpu.VMEM((tq, d), BF16)],
        compiler_params=pltpu.CompilerParams(dimension_semantics=("arbitrary",),
                                             vmem_limit_bytes=V7X_VMEM_LIMIT_BYTES),
        name=name,
    )(*args, *[a for a, _ in casts])
    return out[0], out[1:]


def _rope_tables(n_lat):
    half = HEAD_DIM // 2
    rows_n = n_lat // GRID_W
    row = np.repeat(np.arange(rows_n), GRID_W).astype(np.float32)
    col = np.tile(np.arange(GRID_W), rows_n).astype(np.float32)
    inv = (np.float32(ROPE_THETA) ** (-np.arange(0, half, 2, dtype=np.float32) / np.float32(half))).astype(np.float32)
    ang_r = row[:, None] * inv[None, :]
    ang_c = col[:, None] * inv[None, :]
    cr, sr, cc, sc = np.cos(ang_r), np.sin(ang_r), np.cos(ang_c), np.sin(ang_c)
    zero = np.zeros_like(sr)
    cos = np.concatenate([cr, cr, cc, cc], axis=-1)
    sin_a = np.concatenate([-sr, zero, -sc, zero], axis=-1)
    sin_b = np.concatenate([zero, sr, zero, sc], axis=-1)
    return jnp.asarray(cos), jnp.asarray(sin_a), jnp.asarray(sin_b)


def kernel(x_prompt, x_sample, cache_k, cache_v, c, c_ctx, mod_w, mod_b, norm_mix_pre, norm_mix_post, norm_ffn_pre, norm_ffn_post, conv_w_in, conv_k, conv_w_out, attn_w_qkv, attn_q_gain, attn_k_gain, attn_w_o, ffn_w_up, ffn_conv, ffn_w_down):
    batch, seq, d = x_prompt.shape
    dec_batch, dec_seq, _ = x_sample.shape
    depth = mod_w.shape[0]
    past_len = cache_k.shape[2]
    t_ctx = 2 * seq
    tq_ctx = 2 * seq
    t_lat_conv = 1024
    t_lat_qkv = 512
    tq_lat = 256

    n_cond = dec_batch + 1
    pad = (-n_cond) % V7X_BF16_ROWS
    cvec = jnp.concatenate([c, c_ctx[None, :], jnp.zeros((pad, d), F32)], axis=0)
    mod_all = _modulation(cvec, mod_w, mod_b)

    per_layer_rows = lambda a: a.reshape(a.shape[0], 1, a.shape[1])
    g_mix_pre, g_mix_post = per_layer_rows(norm_mix_pre), per_layer_rows(norm_mix_post)
    g_ffn_pre, g_ffn_post = per_layer_rows(norm_ffn_pre), per_layer_rows(norm_ffn_post)
    q_gain, k_gain = per_layer_rows(attn_q_gain), per_layer_rows(attn_k_gain)
    w_in, w_out = conv_w_in.astype(BF16), conv_w_out.astype(BF16)
    w_qkv, w_o = attn_w_qkv.astype(BF16), attn_w_o.astype(BF16)

    h_ctx = x_prompt.reshape(batch * seq, d)
    h_lat = x_sample.reshape(dec_batch * dec_seq, d)
    rope_tables = _rope_tables(dec_seq)
    new_k, new_v = [], []

    for i in range(depth):
        j = i // 2
        mod_lat = mod_all[i, :dec_batch][:, None, :]
        mod_ctx = mod_all[i, dec_batch:dec_batch + 1][:, None, :]
        if i % 2 == 0:
            h_ctx, (w_down,) = _mixer(f"mixer{i}_ctx", (i, j, j), h_ctx, mod_ctx, g_mix_pre, g_mix_post, w_in, conv_k,
                                      w_out, seq, t_ctx, casts=[(ffn_w_down, i)])
            h_lat, (w_up,) = _mixer(f"mixer{i}_lat", (i, j, j), h_lat, mod_lat, g_mix_pre, g_mix_post, w_in, conv_k,
                                    w_out, dec_seq, t_lat_conv, casts=[(ffn_w_up, i)])
        else:
            (q_c, k_c, v_c, k_c32, v_c32), (w_down,) = _qkv(
                f"qkv{i}_ctx", (i, j), h_ctx, mod_ctx, g_mix_pre, w_qkv, q_gain, k_gain, seq, t_ctx,
                emit_f32_kv=True, casts=[(ffn_w_down, i)])
            new_k.append(k_c32.reshape(batch, seq, N_KV_HEADS, HEAD_DIM))
            new_v.append(v_c32.reshape(batch, seq, N_KV_HEADS, HEAD_DIM))
            h_ctx, (w_up,) = _attention(f"attn{i}_ctx", (i, j), h_ctx, q_c, [(k_c, v_c, seq)], mod_ctx, g_mix_post, w_o,
                                        seq, tq_ctx, casts=[(ffn_w_up, i)])
            (q_l, k_l, v_l), _ = _qkv(f"qkv{i}_lat", (i, j), h_lat, mod_lat, g_mix_pre, w_qkv, q_gain, k_gain,
                                      dec_seq, t_lat_qkv, rope_tables=rope_tables)
            ck = cache_k[:, j].reshape(dec_batch * past_len * N_KV_HEADS, HEAD_DIM)
            cv = cache_v[:, j].reshape(dec_batch * past_len * N_KV_HEADS, HEAD_DIM)
            h_lat, _ = _attention(f"attn{i}_lat", (i, j), h_lat, q_l, [(ck, cv, past_len), (k_l, v_l, dec_seq)],
                                  mod_lat, g_mix_post, w_o, dec_seq, tq_lat)
        h_ctx = _ffn(f"ffn{i}_ctx", (i, 0, i), h_ctx, mod_ctx, g_ffn_pre, g_ffn_post, w_up[None], ffn_conv, w_down[None],
                     seq, t_ctx)
        h_lat = _ffn(f"ffn{i}_lat", (i, 0, i), h_lat, mod_lat, g_ffn_pre, g_ffn_post, w_up[None], ffn_conv, w_down[None],
                     dec_seq, t_lat_conv)

    return (h_ctx.reshape(batch, seq, d), h_lat.reshape(dec_batch, dec_seq, d),
            jnp.stack(new_k, axis=1), jnp.stack(new_v, axis=1))
```

```python
import functools

import jax
import jax.numpy as jnp
import numpy as np
from jax import lax
from jax.experimental import pallas as pl
from jax.experimental.pallas import tpu as pltpu

EPS = 1e-6
ROPE_THETA = 10000.0
GRID_W = 64
HEAD_DIM = 128
N_KV_HEADS = 2

V7X_SUBLANES = 8
V7X_BF16_ROWS = 16
V7X_VMEM_LIMIT_BYTES = 56 * 1024 * 1024

HALO = V7X_SUBLANES
CONV_CHUNK = 256
KEY_BLOCK = 1024
LOG2_E = 1.4426950408889634

BF16 = jnp.bfloat16
F32 = jnp.float32


def _resident(block_shape, index_map):
    return pl.BlockSpec(block_shape, index_map, pipeline_mode=pl.Buffered(1))


def _layer_spec(stacked, layer):
    zeros = (0,) * (stacked.ndim - 1)
    return _resident((None,) + stacked.shape[1:], lambda i: (layer,) + zeros)


def _with_side_casts(kernel_fn, n_in, n_out, n_casts):
    if n_casts == 0:
        return kernel_fn

    def body(*refs):
        ins, rest = refs[:n_in], refs[n_in:]
        srcs, rest = rest[:n_casts], rest[n_casts:]
        outs, rest = rest[:n_out], rest[n_out:]
        dsts, scratch = rest[:n_casts], rest[n_casts:]
        for src, dst in zip(srcs, dsts):
            dst[...] = src[...].astype(BF16)
        kernel_fn(*ins, *outs, *scratch)

    return body


def _cast_specs(casts, steps):
    in_specs, out_specs, out_shape = [], [], []
    for stacked, layer in casts:
        _, rows, cols = stacked.shape
        assert rows % (steps * V7X_BF16_ROWS) == 0
        blk = rows // steps
        in_specs.append(pl.BlockSpec((None, blk, cols), lambda i, layer=layer: (layer, i, 0)))
        out_specs.append(pl.BlockSpec((blk, cols), lambda i: (i, 0)))
        out_shape.append(jax.ShapeDtypeStruct((rows, cols), BF16))
    return in_specs, out_specs, out_shape


def _rms(x):
    return x * lax.rsqrt(jnp.mean(x * x, axis=-1, keepdims=True) + EPS)


def _dot(a, b):
    return jnp.dot(a, b, preferred_element_type=F32)


def _dot_nt(a, b):
    return lax.dot_general(a, b, (((1,), (1,)), ((), ())), preferred_element_type=F32)


def _mod_kernel(c_ref, w_ref, b_ref, o_ref):
    c = c_ref[...]
    s = c * (1.0 / (1.0 + jnp.exp(-c)))
    o_ref[0] = _dot(s.astype(BF16), w_ref[0].astype(BF16)) + b_ref[0]


def _modulation(cvec, mod_w, mod_b):
    depth, d, n = mod_w.shape
    rows = cvec.shape[0]
    tn = 1536
    return pl.pallas_call(
        _mod_kernel,
        grid=(depth, n // tn),
        in_specs=[
            pl.BlockSpec((rows, d), lambda l, j: (0, 0)),
            pl.BlockSpec((1, d, tn), lambda l, j: (l, 0, j)),
            pl.BlockSpec((1, 1, tn), lambda l, j: (l, 0, j)),
        ],
        out_specs=pl.BlockSpec((1, rows, tn), lambda l, j: (l, 0, j)),
        out_shape=jax.ShapeDtypeStruct((depth, rows, n), F32),
        compiler_params=pltpu.CompilerParams(dimension_semantics=("arbitrary", "arbitrary")),
        name="modulation",
    )(cvec, mod_w, mod_b.reshape(depth, 1, n))


def _fill_u(refs, has_halo, mod_ref, gpre_ref, ubuf, t, d, mod_off):
    shift = mod_ref[0, :, mod_off * d:(mod_off + 1) * d]
    scale = mod_ref[0, :, (mod_off + 1) * d:(mod_off + 2) * d]
    gs = gpre_ref[...] * (1.0 + scale)
    if has_halo:
        xp_ref, x_ref, xn_ref = refs
        halo = jnp.concatenate([xp_ref[...], xn_ref[...]], axis=0)
        ubuf[t:t + 2 * HALO, :] = (_rms(halo) * gs + shift).astype(BF16)
    else:
        (x_ref,) = refs
    ubuf[0:t, :] = (_rms(x_ref[...]) * gs + shift).astype(BF16)
    return x_ref


def _edge_flags(tiles_per_seq):
    i = pl.program_id(0)
    pos = lax.rem(i, tiles_per_seq)
    return (pos != 0).astype(F32), (pos != tiles_per_seq - 1).astype(F32)


def _row_masks(t, rows_per_seq):
    pos = lax.rem(lax.broadcasted_iota(jnp.int32, (t, CONV_CHUNK), 0), rows_per_seq)
    return pos == 0, pos == rows_per_seq - 1


def _conv3(h, k, t, has_halo, flags, first_row, last_row):
    zm = h[0:t]
    if has_halo:
        zp = h[t + HALO - 1:t + HALO] * flags[0]
        zn = h[t + HALO:t + HALO + 1] * flags[1]
    else:
        zp = jnp.zeros((1, h.shape[1]), F32)
        zn = zp
    z_prev = jnp.where(first_row, zp, pltpu.roll(zm, 1, 0))
    z_next = jnp.where(last_row, zn, pltpu.roll(zm, t - 1, 0))
    return z_prev * k[0:1] + zm * k[1:2] + z_next * k[2:3]


def _finish(x_ref, y, mod_ref, gpost_ref, o_ref, d, gate_off):
    gate = mod_ref[0, :, gate_off * d:(gate_off + 1) * d]
    o_ref[...] = x_ref[...] + (gate * gpost_ref[...]) * _rms(y)


def _mixer_kernel(*refs, t, d, has_halo, tiles_per_seq, rows_per_seq):
    n_x = 3 if has_halo else 1
    x_refs = refs[:n_x]
    mod_ref, gpre_ref, gpost_ref, win_ref, ck_ref, wout_ref, o_ref, ubuf, ybuf = refs[n_x:]
    x_ref = _fill_u(x_refs, has_halo, mod_ref, gpre_ref, ubuf, t, d, 0)
    flags = _edge_flags(tiles_per_seq) if has_halo else None
    first_row, last_row = _row_masks(t, rows_per_seq)
    for j in range(d // CONV_CHUNK):
        c0 = j * CONV_CHUNK
        b_gate = _dot(ubuf[0:t, :], win_ref[:, c0:c0 + CONV_CHUNK])
        c_gate = _dot(ubuf[...], win_ref[:, d + c0:d + c0 + CONV_CHUNK])
        xp = _dot(ubuf[...], win_ref[:, 2 * d + c0:2 * d + c0 + CONV_CHUNK])
        conv = _conv3(c_gate * xp, ck_ref[:, c0:c0 + CONV_CHUNK], t, has_halo, flags, first_row, last_row)
        ybuf[:, c0:c0 + CONV_CHUNK] = (b_gate * conv).astype(BF16)
    y = _dot(ybuf[...], wout_ref[...])
    _finish(x_ref, y, mod_ref, gpost_ref, o_ref, d, 2)


def _ffn_kernel(*refs, t, d, d_ff, has_halo, tiles_per_seq, rows_per_seq):
    n_x = 3 if has_halo else 1
    x_refs = refs[:n_x]
    mod_ref, gpre_ref, gpost_ref, wup_ref, ck_ref, wdown_ref, o_ref, ubuf, abuf = refs[n_x:]
    x_ref = _fill_u(x_refs, has_halo, mod_ref, gpre_ref, ubuf, t, d, 3)
    flags = _edge_flags(tiles_per_seq) if has_halo else None
    first_row, last_row = _row_masks(t, rows_per_seq)
    for j in range(d_ff // CONV_CHUNK):
        c0 = j * CONV_CHUNK
        hg = _dot(ubuf[...], wup_ref[:, c0:c0 + CONV_CHUNK])
        hu = _dot(ubuf[...], wup_ref[:, d_ff + c0:d_ff + c0 + CONV_CHUNK])
        g = _conv3(hg, ck_ref[:, c0:c0 + CONV_CHUNK], t, has_halo, flags, first_row, last_row)
        u = _conv3(hu, ck_ref[:, d_ff + c0:d_ff + c0 + CONV_CHUNK], t, has_halo, flags, first_row, last_row)
        act = g * (1.0 / (1.0 + jnp.exp(-g))) * u
        abuf[:, c0:c0 + CONV_CHUNK] = act.astype(BF16)
    y = _dot(abuf[...], wdown_ref[...])
    _finish(x_ref, y, mod_ref, gpost_ref, o_ref, d, 5)


def _conv_sublayer(kernel_fn, name, layer, x, mod, gpre, gpost, w_a, ck, w_b, seq_len, t, casts=()):
    rows, d = x.shape
    assert rows % t == 0 and t % V7X_BF16_ROWS == 0 and (seq_len % t == 0 or t % seq_len == 0)
    tiles_per_seq = max(seq_len // t, 1)
    has_halo = t < seq_len
    per_seq_mod = mod.shape[0] > 1
    assert not per_seq_mod or t <= seq_len
    hb = t // HALO
    last_hb = rows // HALO - 1

    def mod_map(i):
        return ((i // tiles_per_seq) if per_seq_mod else 0, 0, 0)

    x_specs = [pl.BlockSpec((t, d), lambda i: (i, 0))]
    x_args = [x]
    if has_halo:
        x_specs = [pl.BlockSpec((HALO, d), lambda i: (jnp.maximum(i * hb - 1, 0), 0)),
                   x_specs[0],
                   pl.BlockSpec((HALO, d), lambda i: (jnp.minimum((i + 1) * hb, last_hb), 0))]
        x_args = [x, x, x]
    m_rows = t + 2 * HALO if has_halo else t
    body = functools.partial(kernel_fn, t=t, d=d, has_halo=has_halo, tiles_per_seq=tiles_per_seq,
                             rows_per_seq=min(t, seq_len))
    in_specs = x_specs + [
        pl.BlockSpec((1, 1, mod.shape[2]), mod_map),
        _layer_spec(gpre, layer[0]), _layer_spec(gpost, layer[0]),
        _layer_spec(w_a, layer[1]), _layer_spec(ck, layer[2]), _layer_spec(w_b, layer[1]),
    ]
    cast_in, cast_out, cast_shape = _cast_specs(casts, rows // t)
    out = pl.pallas_call(
        _with_side_casts(body, len(in_specs), 1, len(casts)),
        grid=(rows // t,),
        in_specs=in_specs + cast_in,
        out_specs=[pl.BlockSpec((t, d), lambda i: (i, 0))] + cast_out,
        out_shape=[jax.ShapeDtypeStruct((rows, d), F32)] + cast_shape,
        scratch_shapes=[pltpu.VMEM((m_rows, d), BF16), pltpu.VMEM((t, w_b.shape[1]), BF16)],
        compiler_params=pltpu.CompilerParams(dimension_semantics=("arbitrary",),
                                             vmem_limit_bytes=V7X_VMEM_LIMIT_BYTES),
        name=name,
    )(*x_args, mod, gpre, gpost, w_a, ck, w_b, *[a for a, _ in casts])
    return out[0], out[1:]


def _mixer(name, layer, x, mod, gpre, gpost, w_in, ck, w_out, seq_len, t, casts=()):
    return _conv_sublayer(_mixer_kernel, name, layer, x, mod, gpre, gpost, w_in, ck, w_out, seq_len, t, casts)


def _ffn(name, layer, x, mod, gpre, gpost, w_up, ck, w_down, seq_len, t):
    kernel_fn = functools.partial(_ffn_kernel, d_ff=w_down.shape[1])
    return _conv_sublayer(kernel_fn, name, layer, x, mod, gpre, gpost, w_up, ck, w_down, seq_len, t)[0]


def _qkv_kernel(*refs, d, n_heads, rope, emit_f32_kv):
    x_ref, mod_ref, gpre_ref, w_ref, qg_ref, kg_ref = refs[:6]
    refs = refs[6:]
    if rope:
        cos_ref, sina_ref, sinb_ref = refs[:3]
        refs = refs[3:]
    q_ref, k_ref, v_ref = refs[:3]
    n_kv32 = 2 if emit_f32_kv else 0
    kv32_refs = refs[3:3 + n_kv32]
    t = x_ref.shape[0]
    shift = mod_ref[0, :, 0:d]
    scale = mod_ref[0, :, d:2 * d]
    u = (_rms(x_ref[...]) * (gpre_ref[...] * (1.0 + scale)) + shift).astype(BF16)
    q_scale = HEAD_DIM ** -0.5 * LOG2_E
    pair = 2 * HEAD_DIM
    n_pairs = (n_heads + N_KV_HEADS) // 2

    if rope:
        (ubuf,) = refs[3 + n_kv32:]
        ubuf[...] = u
        project = lambda p: _dot(ubuf[...], w_ref[:, p * pair:(p + 1) * pair])
        in_flight = 2
    else:
        qkv = _dot(u, w_ref[...])
        project = lambda p: qkv[:, p * pair:(p + 1) * pair]
        in_flight = n_pairs + 1

    def issue_after(done):
        tile = (slice(0, V7X_BF16_ROWS), slice(0, HEAD_DIM))
        blk = done[0:V7X_BF16_ROWS, :]
        ubuf[tile] = ubuf[tile] + (blk - blk).astype(BF16)

    ones = jnp.ones((HEAD_DIM, HEAD_DIM), BF16)

    def head(hv, gain):
        if rope:
            ms = _dot((hv * hv).astype(BF16), ones) * (1.0 / HEAD_DIM)
            hv = hv * lax.rsqrt(ms + EPS) * gain
            hv = (hv * cos_ref[...]
                  + pltpu.roll(hv, HEAD_DIM - HEAD_DIM // 4, 1) * sina_ref[...]
                  + pltpu.roll(hv, HEAD_DIM // 4, 1) * sinb_ref[...])
        else:
            hv = _rms(hv) * gain
        return hv

    raw = {p: project(p) for p in range(min(in_flight, n_pairs + 1))}
    for p in range(n_pairs):
        first = None
        for idx in (2 * p, 2 * p + 1):
            lanes = slice((idx % 2) * HEAD_DIM, (idx % 2 + 1) * HEAD_DIM)
            if idx < n_heads:
                hv = head(raw[p][:, lanes], qg_ref[...])
                q_ref[:, idx * HEAD_DIM:(idx + 1) * HEAD_DIM] = (hv * q_scale).astype(BF16)
            else:
                hk = idx - n_heads
                hv = head(raw[p][:, lanes], kg_ref[...])
                k_ref[:, hk * HEAD_DIM:(hk + 1) * HEAD_DIM] = hv.astype(BF16)
                if emit_f32_kv:
                    kv32_refs[0][pl.ds(hk, t, stride=N_KV_HEADS), :] = hv
            first = hv if first is None else first
        del raw[p]
        if p + in_flight <= n_pairs:
            issue_after(first)
            raw[p + in_flight] = project(p + in_flight)
    v = raw[n_pairs]
    for hk in range(N_KV_HEADS):
        lanes = slice(hk * HEAD_DIM, (hk + 1) * HEAD_DIM)
        v_ref[:, lanes] = v[:, lanes].astype(BF16)
        if emit_f32_kv:
            kv32_refs[1][pl.ds(hk, t, stride=N_KV_HEADS), :] = v[:, lanes]


def _qkv(name, layer, x, mod, gpre, w_qkv, q_gain, k_gain, seq_len, t, rope_tables=None, emit_f32_kv=False,
         casts=()):
    rows, d = x.shape
    n_heads = d // HEAD_DIM
    kv_w = N_KV_HEADS * HEAD_DIM
    tiles_per_seq = max(seq_len // t, 1)
    per_seq_mod = mod.shape[0] > 1
    rope = rope_tables is not None
    assert not (per_seq_mod or rope) or seq_len % t == 0

    def mod_map(i):
        return ((i // tiles_per_seq) if per_seq_mod else 0, 0, 0)

    in_specs = [
        pl.BlockSpec((t, d), lambda i: (i, 0)),
        pl.BlockSpec((1, 1, mod.shape[2]), mod_map),
        _layer_spec(gpre, layer[0]), _layer_spec(w_qkv, layer[1]),
        _layer_spec(q_gain, layer[1]), _layer_spec(k_gain, layer[1]),
    ]
    args = [x, mod, gpre, w_qkv, q_gain, k_gain]
    if rope:
        in_specs += [pl.BlockSpec((t, HEAD_DIM), lambda i: (i % tiles_per_seq, 0))] * 3
        args += list(rope_tables)
    out_specs = [pl.BlockSpec((t, d), lambda i: (i, 0)),
                 pl.BlockSpec((t, kv_w), lambda i: (i, 0)),
                 pl.BlockSpec((t, kv_w), lambda i: (i, 0))]
    out_shape = [jax.ShapeDtypeStruct((rows, d), BF16),
                 jax.ShapeDtypeStruct((rows, kv_w), BF16),
                 jax.ShapeDtypeStruct((rows, kv_w), BF16)]
    if emit_f32_kv:
        out_specs += [pl.BlockSpec((N_KV_HEADS * t, HEAD_DIM), lambda i: (i, 0))] * 2
        out_shape += [jax.ShapeDtypeStruct((N_KV_HEADS * rows, HEAD_DIM), F32)] * 2
    cast_in, cast_out, cast_shape = _cast_specs(casts, rows // t)
    body = functools.partial(_qkv_kernel, d=d, n_heads=n_heads, rope=rope, emit_f32_kv=emit_f32_kv)
    out = pl.pallas_call(
        _with_side_casts(body, len(in_specs), len(out_specs), len(casts)),
        grid=(rows // t,),
        in_specs=in_specs + cast_in,
        out_specs=out_specs + cast_out,
        out_shape=out_shape + cast_shape,
        scratch_shapes=[pltpu.VMEM((t, d), BF16)] if rope else [],
        compiler_params=pltpu.CompilerParams(dimension_semantics=("arbitrary",),
                                             vmem_limit_bytes=V7X_VMEM_LIMIT_BYTES),
        name=name,
    )(*args, *[a for a, _ in casts])
    return out[:len(out_specs)], out[len(out_specs):]


def _kv_block(ref, kk, b0, kb):
    if ref.shape[1] == HEAD_DIM:
        return ref[pl.ds(N_KV_HEADS * b0 + kk, kb, stride=N_KV_HEADS), :].astype(BF16)
    return ref[b0:b0 + kb, kk * HEAD_DIM:(kk + 1) * HEAD_DIM]


def _attn_kernel(*refs, d, n_heads, n_kv_sets, seqs_per_tile):
    x_ref, q_ref = refs[:2]
    kv_refs = refs[2:2 + 2 * n_kv_sets]
    mod_ref, gpost_ref, wo_ref, o_ref, obuf = refs[2 + 2 * n_kv_sets:]
    tq = q_ref.shape[0] // seqs_per_tile
    group = n_heads // N_KV_HEADS
    for sub, kk in [(a, b) for a in range(seqs_per_tile) for b in range(N_KV_HEADS)]:
        rows = slice(sub * tq, (sub + 1) * tq)
        qs = jnp.concatenate(
            [q_ref[rows, (kk * group + g) * HEAD_DIM:(kk * group + g + 1) * HEAD_DIM] for g in range(group)], axis=0)
        m = denom = acc = None
        for s in range(n_kv_sets):
            k_ref, v_ref = kv_refs[2 * s], kv_refs[2 * s + 1]
            n_keys = k_ref.shape[0] * k_ref.shape[1] // (N_KV_HEADS * HEAD_DIM * seqs_per_tile)
            kb = min(KEY_BLOCK, n_keys)
            for b0 in range(sub * n_keys, (sub + 1) * n_keys, kb):
                sc = _dot_nt(qs, _kv_block(k_ref, kk, b0, kb))
                v_blk = _kv_block(v_ref, kk, b0, kb)
                mb = sc.max(axis=-1, keepdims=True)
                if m is None:
                    m = mb
                    p = jnp.exp2(sc - m)
                    denom = p.sum(axis=-1, keepdims=True)
                    acc = _dot(p.astype(BF16), v_blk)
                else:
                    m_new = jnp.maximum(m, mb)
                    alpha = jnp.exp2(m - m_new)
                    p = jnp.exp2(sc - m_new)
                    denom = alpha * denom + p.sum(axis=-1, keepdims=True)
                    acc = alpha * acc + _dot(p.astype(BF16), v_blk)
                    m = m_new
        out = acc * (1.0 / denom)
        for g in range(group):
            hq = kk * group + g
            obuf[rows, hq * HEAD_DIM:(hq + 1) * HEAD_DIM] = out[g * tq:(g + 1) * tq].astype(BF16)
    y = _dot(obuf[...], wo_ref[...])
    _finish(x_ref, y, mod_ref, gpost_ref, o_ref, d, 2)


def _attention(name, layer, x, q, kv_sets, mod, gpost, w_o, seq_len, tq, casts=()):
    rows, d = x.shape
    n_heads = d // HEAD_DIM
    kv_w = N_KV_HEADS * HEAD_DIM
    assert seq_len % tq == 0 or tq % seq_len == 0
    tiles_per_seq = max(seq_len // tq, 1)
    seqs_per_tile = max(tq // seq_len, 1)
    per_seq_mod = mod.shape[0] > 1
    assert not per_seq_mod or seqs_per_tile == 1

    def mod_map(i):
        return ((i // tiles_per_seq) if per_seq_mod else 0, 0, 0)

    in_specs = [pl.BlockSpec((tq, d), lambda i: (i, 0)), pl.BlockSpec((tq, d), lambda i: (i, 0))]
    args = [x, q]
    for k, v, n in kv_sets:
        block = (seqs_per_tile * n * kv_w // k.shape[1], k.shape[1])
        in_specs += [pl.BlockSpec(block, lambda i: (i // tiles_per_seq, 0))] * 2
        args += [k, v]
    in_specs += [pl.BlockSpec((1, 1, mod.shape[2]), mod_map),
                 _layer_spec(gpost, layer[0]), _layer_spec(w_o, layer[1])]
    args += [mod, gpost, w_o]
    cast_in, cast_out, cast_shape = _cast_specs(casts, rows // tq)
    body = functools.partial(_attn_kernel, d=d, n_heads=n_heads, n_kv_sets=len(kv_sets), seqs_per_tile=seqs_per_tile)
    out = pl.pallas_call(
        _with_side_casts(body, len(in_specs), 1, len(casts)),
        grid=(rows // tq,),
        in_specs=in_specs + cast_in,
        out_specs=[pl.BlockSpec((tq, d), lambda i: (i, 0))] + cast_out,
        out_shape=[jax.ShapeDtypeStruct((rows, d), F32)] + cast_shape,
        scratch_shapes=[pltpu.VMEM((tq, d), BF16)],
        compiler_params=pltpu.CompilerParams(dimension_semantics=("arbitrary",),
                                             vmem_limit_bytes=V7X_VMEM_LIMIT_BYTES),
        name=name,
    )(*args, *[a for a, _ in casts])
    return out[0], out[1:]


def _rope_tables(n_lat):
    half = HEAD_DIM // 2
    rows_n = n_lat // GRID_W
    row = np.repeat(np.arange(rows_n), GRID_W).astype(np.float32)
    col = np.tile(np.arange(GRID_W), rows_n).astype(np.float32)
    inv = (np.float32(ROPE_THETA) ** (-np.arange(0, half, 2, dtype=np.float32) / np.float32(half))).astype(np.float32)
    ang_r = row[:, None] * inv[None, :]
    ang_c = col[:, None] * inv[None, :]
    cr, sr, cc, sc = np.cos(ang_r), np.sin(ang_r), np.cos(ang_c), np.sin(ang_c)
    zero = np.zeros_like(sr)
    cos = np.concatenate([cr, cr, cc, cc], axis=-1)
    sin_a = np.concatenate([-sr, zero, -sc, zero], axis=-1)
    sin_b = np.concatenate([zero, sr, zero, sc], axis=-1)
    return jnp.asarray(cos), jnp.asarray(sin_a), jnp.asarray(sin_b)


def kernel(x_prompt, x_sample, cache_k, cache_v, c, c_ctx, mod_w, mod_b, norm_mix_pre, norm_mix_post, norm_ffn_pre, norm_ffn_post, conv_w_in, conv_k, conv_w_out, attn_w_qkv, attn_q_gain, attn_k_gain, attn_w_o, ffn_w_up, ffn_conv, ffn_w_down):
    batch, seq, d = x_prompt.shape
    dec_batch, dec_seq, _ = x_sample.shape
    depth = mod_w.shape[0]
    past_len = cache_k.shape[2]
    t_ctx = 4 * seq
    tq_ctx = 2 * seq
    t_lat = 1024
    tq_lat = 256

    n_cond = dec_batch + 1
    pad = (-n_cond) % V7X_BF16_ROWS
    cvec = jnp.concatenate([c, c_ctx[None, :], jnp.zeros((pad, d), F32)], axis=0)
    mod_all = _modulation(cvec, mod_w, mod_b)

    per_layer_rows = lambda a: a.reshape(a.shape[0], 1, a.shape[1])
    g_mix_pre, g_mix_post = per_layer_rows(norm_mix_pre), per_layer_rows(norm_mix_post)
    g_ffn_pre, g_ffn_post = per_layer_rows(norm_ffn_pre), per_layer_rows(norm_ffn_post)
    q_gain, k_gain = per_layer_rows(attn_q_gain), per_layer_rows(attn_k_gain)
    w_in, w_out = conv_w_in.astype(BF16), conv_w_out.astype(BF16)
    w_qkv, w_o = attn_w_qkv.astype(BF16), attn_w_o.astype(BF16)

    h_ctx = x_prompt.reshape(batch * seq, d)
    h_lat = x_sample.reshape(dec_batch * dec_seq, d)
    rope_tables = _rope_tables(dec_seq)
    new_k, new_v = [], []

    for i in range(depth):
        j = i // 2
        mod_lat = mod_all[i, :dec_batch][:, None, :]
        mod_ctx = mod_all[i, dec_batch:dec_batch + 1][:, None, :]
        if i % 2 == 0:
            h_ctx, (w_down,) = _mixer(f"mixer{i}_ctx", (i, j, j), h_ctx, mod_ctx, g_mix_pre, g_mix_post, w_in, conv_k,
                                      w_out, seq, t_ctx, casts=[(ffn_w_down, i)])
            h_lat, (w_up,) = _mixer(f"mixer{i}_lat", (i, j, j), h_lat, mod_lat, g_mix_pre, g_mix_post, w_in, conv_k,
                                    w_out, dec_seq, t_lat, casts=[(ffn_w_up, i)])
        else:
            (q_c, k_c, v_c, k_c32, v_c32), (w_down,) = _qkv(
                f"qkv{i}_ctx", (i, j), h_ctx, mod_ctx, g_mix_pre, w_qkv, q_gain, k_gain, seq, t_ctx,
                emit_f32_kv=True, casts=[(ffn_w_down, i)])
            new_k.append(k_c32.reshape(batch, seq, N_KV_HEADS, HEAD_DIM))
            new_v.append(v_c32.reshape(batch, seq, N_KV_HEADS, HEAD_DIM))
            h_ctx, (w_up,) = _attention(f"attn{i}_ctx", (i, j), h_ctx, q_c, [(k_c, v_c, seq)], mod_ctx, g_mix_post, w_o,
                                        seq, tq_ctx, casts=[(ffn_w_up, i)])
            (q_l, k_l, v_l), _ = _qkv(f"qkv{i}_lat", (i, j), h_lat, mod_lat, g_mix_pre, w_qkv, q_gain, k_gain,
                                      dec_seq, t_lat, rope_tables=rope_tables)
            ck = cache_k[:, j].reshape(dec_batch * past_len * N_KV_HEADS, HEAD_DIM)
            cv = cache_v[:, j].reshape(dec_batch * past_len * N_KV_HEADS, HEAD_DIM)
            h_lat, _ = _attention(f"attn{i}_lat", (i, j), h_lat, q_l, [(ck, cv, past_len), (k_l, v_l, dec_seq)],
                                  mod_lat, g_mix_post, w_o, dec_seq, tq_lat)
        h_ctx = _ffn(f"ffn{i}_ctx", (i, 0, i), h_ctx, mod_ctx, g_ffn_pre, g_ffn_post, w_up[None], ffn_conv, w_down[None],
                     seq, t_ctx)
        h_lat = _ffn(f"ffn{i}_lat", (i, 0, i), h_lat, mod_lat, g_ffn_pre, g_ffn_post, w_up[None], ffn_conv, w_down[None],
                     dec_seq, t_lat)

    return (h_ctx.reshape(batch, seq, d), h_lat.reshape(dec_batch, dec_seq, d),
            jnp.stack(new_k, axis=1), jnp.stack(new_v, axis=1))
```

```python
import functools

import jax
import jax.numpy as jnp
import numpy as np
from jax import lax
from jax.experimental import pallas as pl
from jax.experimental.pallas import tpu as pltpu

EPS = 1e-6
ROPE_THETA = 10000.0
GRID_W = 64
HEAD_DIM = 128
N_KV_HEADS = 2

V7X_SUBLANES = 8
V7X_BF16_ROWS = 16
V7X_VMEM_LIMIT_BYTES = 56 * 1024 * 1024

HALO = V7X_SUBLANES
CONV_CHUNK = 256
KEY_BLOCK = 1024
MOD_COLS = 1536
LOG2_E = 1.4426950408889634

BF16 = jnp.bfloat16
F32 = jnp.float32


def _resident(block_shape, index_map):
    return pl.BlockSpec(block_shape, index_map, pipeline_mode=pl.Buffered(1))


def _layer_spec(stacked, layer):
    zeros = (0,) * (stacked.ndim - 1)
    return _resident((None,) + stacked.shape[1:], lambda i: (layer,) + zeros)


def _with_side_casts(kernel_fn, n_in, n_out, n_casts):
    if n_casts == 0:
        return kernel_fn

    def body(*refs):
        ins, rest = refs[:n_in], refs[n_in:]
        srcs, rest = rest[:n_casts], rest[n_casts:]
        outs, rest = rest[:n_out], rest[n_out:]
        dsts, scratch = rest[:n_casts], rest[n_casts:]
        for src, dst in zip(srcs, dsts):
            dst[...] = src[...].astype(BF16)
        kernel_fn(*ins, *outs, *scratch)

    return body


def _cast_specs(casts, steps):
    in_specs, out_specs, out_shape = [], [], []
    for stacked, layer in casts:
        _, rows, cols = stacked.shape
        assert rows % (steps * V7X_BF16_ROWS) == 0
        blk = rows // steps
        in_specs.append(pl.BlockSpec((None, blk, cols), lambda i, layer=layer: (layer, i, 0)))
        out_specs.append(pl.BlockSpec((blk, cols), lambda i: (i, 0)))
        out_shape.append(jax.ShapeDtypeStruct((rows, cols), BF16))
    return in_specs, out_specs, out_shape


def _rms(x):
    return x * lax.rsqrt(jnp.mean(x * x, axis=-1, keepdims=True) + EPS)


def _dot(a, b):
    return jnp.dot(a, b, preferred_element_type=F32)


def _dot_nt(a, b):
    return lax.dot_general(a, b, (((1,), (1,)), ((), ())), preferred_element_type=F32)


def _mod_kernel(c_ref, w_ref, b_ref, o_ref):
    c = c_ref[...]
    s = c * (1.0 / (1.0 + jnp.exp(-c)))
    o_ref[0] = _dot(s.astype(BF16), w_ref[0].astype(BF16)) + b_ref[0]


def _modulation(cvec, mod_w, mod_b):
    depth, d, n = mod_w.shape
    rows = cvec.shape[0]
    tn = MOD_COLS
    assert n % tn == 0
    return pl.pallas_call(
        _mod_kernel,
        grid=(depth, n // tn),
        in_specs=[
            pl.BlockSpec((rows, d), lambda l, j: (0, 0)),
            pl.BlockSpec((1, d, tn), lambda l, j: (l, 0, j)),
            pl.BlockSpec((1, 1, tn), lambda l, j: (l, 0, j)),
        ],
        out_specs=pl.BlockSpec((1, rows, tn), lambda l, j: (l, 0, j)),
        out_shape=jax.ShapeDtypeStruct((depth, rows, n), F32),
        compiler_params=pltpu.CompilerParams(dimension_semantics=("arbitrary", "arbitrary")),
        name="modulation",
    )(cvec, mod_w, mod_b.reshape(depth, 1, n))


def _fill_u(refs, has_halo, mod_ref, gpre_ref, ubuf, t, d, mod_off):
    shift = mod_ref[0, :, mod_off * d:(mod_off + 1) * d]
    scale = mod_ref[0, :, (mod_off + 1) * d:(mod_off + 2) * d]
    gs = gpre_ref[...] * (1.0 + scale)
    if has_halo:
        xp_ref, x_ref, xn_ref = refs
        halo = jnp.concatenate([xp_ref[...], xn_ref[...]], axis=0)
        ubuf[t:t + 2 * HALO, :] = (_rms(halo) * gs + shift).astype(BF16)
    else:
        (x_ref,) = refs
    ubuf[0:t, :] = (_rms(x_ref[...]) * gs + shift).astype(BF16)
    return x_ref


def _edge_flags(tiles_per_seq):
    i = pl.program_id(0)
    pos = lax.rem(i, tiles_per_seq)
    return (pos != 0).astype(F32), (pos != tiles_per_seq - 1).astype(F32)


def _row_masks(t, rows_per_seq):
    pos = lax.rem(lax.broadcasted_iota(jnp.int32, (t, CONV_CHUNK), 0), rows_per_seq)
    return pos == 0, pos == rows_per_seq - 1


def _conv3(h, k, t, has_halo, flags, first_row, last_row):
    zm = h[0:t]
    if has_halo:
        zp = h[t + HALO - 1:t + HALO] * flags[0]
        zn = h[t + HALO:t + HALO + 1] * flags[1]
    else:
        zp = jnp.zeros((1, h.shape[1]), F32)
        zn = zp
    z_prev = jnp.where(first_row, zp, pltpu.roll(zm, 1, 0))
    z_next = jnp.where(last_row, zn, pltpu.roll(zm, t - 1, 0))
    return z_prev * k[0:1] + zm * k[1:2] + z_next * k[2:3]


def _finish(x_ref, y, mod_ref, gpost_ref, o_ref, d, gate_off):
    gate = mod_ref[0, :, gate_off * d:(gate_off + 1) * d]
    o_ref[...] = x_ref[...] + (gate * gpost_ref[...]) * _rms(y)


def _mixer_kernel(*refs, t, d, has_halo, tiles_per_seq, rows_per_seq):
    n_x = 3 if has_halo else 1
    x_refs = refs[:n_x]
    mod_ref, gpre_ref, gpost_ref, win_ref, ck_ref, wout_ref, o_ref, ubuf, ybuf = refs[n_x:]
    x_ref = _fill_u(x_refs, has_halo, mod_ref, gpre_ref, ubuf, t, d, 0)
    flags = _edge_flags(tiles_per_seq) if has_halo else None
    first_row, last_row = _row_masks(t, rows_per_seq)
    for j in range(d // CONV_CHUNK):
        c0 = j * CONV_CHUNK
        b_gate = _dot(ubuf[0:t, :], win_ref[:, c0:c0 + CONV_CHUNK])
        c_gate = _dot(ubuf[...], win_ref[:, d + c0:d + c0 + CONV_CHUNK])
        xp = _dot(ubuf[...], win_ref[:, 2 * d + c0:2 * d + c0 + CONV_CHUNK])
        conv = _conv3(c_gate * xp, ck_ref[:, c0:c0 + CONV_CHUNK], t, has_halo, flags, first_row, last_row)
        ybuf[:, c0:c0 + CONV_CHUNK] = (b_gate * conv).astype(BF16)
    y = _dot(ybuf[...], wout_ref[...])
    _finish(x_ref, y, mod_ref, gpost_ref, o_ref, d, 2)


def _ffn_kernel(*refs, t, d, d_ff, has_halo, tiles_per_seq, rows_per_seq):
    n_x = 3 if has_halo else 1
    x_refs = refs[:n_x]
    mod_ref, gpre_ref, gpost_ref, wup_ref, ck_ref, wdown_ref, o_ref, ubuf, abuf = refs[n_x:]
    x_ref = _fill_u(x_refs, has_halo, mod_ref, gpre_ref, ubuf, t, d, 3)
    flags = _edge_flags(tiles_per_seq) if has_halo else None
    first_row, last_row = _row_masks(t, rows_per_seq)
    for j in range(d_ff // CONV_CHUNK):
        c0 = j * CONV_CHUNK
        hg = _dot(ubuf[...], wup_ref[:, c0:c0 + CONV_CHUNK])
        hu = _dot(ubuf[...], wup_ref[:, d_ff + c0:d_ff + c0 + CONV_CHUNK])
        g = _conv3(hg, ck_ref[:, c0:c0 + CONV_CHUNK], t, has_halo, flags, first_row, last_row)
        u = _conv3(hu, ck_ref[:, d_ff + c0:d_ff + c0 + CONV_CHUNK], t, has_halo, flags, first_row, last_row)
        act = g * (1.0 / (1.0 + jnp.exp(-g))) * u
        abuf[:, c0:c0 + CONV_CHUNK] = act.astype(BF16)
    y = _dot(abuf[...], wdown_ref[...])
    _finish(x_ref, y, mod_ref, gpost_ref, o_ref, d, 5)


def _conv_sublayer(kernel_fn, name, layer, x, mod, gpre, gpost, w_a, ck, w_b, seq_len, t, casts=()):
    rows, d = x.shape
    assert rows % t == 0 and t % V7X_BF16_ROWS == 0 and (seq_len % t == 0 or t % seq_len == 0)
    tiles_per_seq = max(seq_len // t, 1)
    has_halo = t < seq_len
    per_seq_mod = mod.shape[0] > 1
    assert not per_seq_mod or t <= seq_len
    hb = t // HALO
    last_hb = rows // HALO - 1

    def mod_map(i):
        return ((i // tiles_per_seq) if per_seq_mod else 0, 0, 0)

    x_specs = [pl.BlockSpec((t, d), lambda i: (i, 0))]
    x_args = [x]
    if has_halo:
        x_specs = [pl.BlockSpec((HALO, d), lambda i: (jnp.maximum(i * hb - 1, 0), 0)),
                   x_specs[0],
                   pl.BlockSpec((HALO, d), lambda i: (jnp.minimum((i + 1) * hb, last_hb), 0))]
        x_args = [x, x, x]
    m_rows = t + 2 * HALO if has_halo else t
    body = functools.partial(kernel_fn, t=t, d=d, has_halo=has_halo, tiles_per_seq=tiles_per_seq,
                             rows_per_seq=min(t, seq_len))
    in_specs = x_specs + [
        pl.BlockSpec((1, 1, mod.shape[2]), mod_map),
        _layer_spec(gpre, layer[0]), _layer_spec(gpost, layer[0]),
        _layer_spec(w_a, layer[1]), _layer_spec(ck, layer[2]), _layer_spec(w_b, layer[1]),
    ]
    cast_in, cast_out, cast_shape = _cast_specs(casts, rows // t)
    out = pl.pallas_call(
        _with_side_casts(body, len(in_specs), 1, len(casts)),
        grid=(rows // t,),
        in_specs=in_specs + cast_in,
        out_specs=[pl.BlockSpec((t, d), lambda i: (i, 0))] + cast_out,
        out_shape=[jax.ShapeDtypeStruct((rows, d), F32)] + cast_shape,
        scratch_shapes=[pltpu.VMEM((m_rows, d), BF16), pltpu.VMEM((t, w_b.shape[1]), BF16)],
        compiler_params=pltpu.CompilerParams(dimension_semantics=("arbitrary",),
                                             vmem_limit_bytes=V7X_VMEM_LIMIT_BYTES),
        name=name,
    )(*x_args, mod, gpre, gpost, w_a, ck, w_b, *[a for a, _ in casts])
    return out[0], out[1:]


def _mixer(name, layer, x, mod, gpre, gpost, w_in, ck, w_out, seq_len, t, casts=()):
    return _conv_sublayer(_mixer_kernel, name, layer, x, mod, gpre, gpost, w_in, ck, w_out, seq_len, t, casts)


def _ffn(name, layer, x, mod, gpre, gpost, w_up, ck, w_down, seq_len, t):
    kernel_fn = functools.partial(_ffn_kernel, d_ff=w_down.shape[1])
    return _conv_sublayer(kernel_fn, name, layer, x, mod, gpre, gpost, w_up, ck, w_down, seq_len, t)[0]


def _qkv_kernel(*refs, d, n_heads, rope, emit_f32_kv):
    x_ref, mod_ref, gpre_ref, w_ref, qg_ref, kg_ref = refs[:6]
    refs = refs[6:]
    if rope:
        cos_ref, sina_ref, sinb_ref = refs[:3]
        refs = refs[3:]
    q_ref, k_ref, v_ref = refs[:3]
    n_kv32 = 2 if emit_f32_kv else 0
    kv32_refs = refs[3:3 + n_kv32]
    t = x_ref.shape[0]
    shift = mod_ref[0, :, 0:d]
    scale = mod_ref[0, :, d:2 * d]
    u = (_rms(x_ref[...]) * (gpre_ref[...] * (1.0 + scale)) + shift).astype(BF16)
    q_scale = HEAD_DIM ** -0.5 * LOG2_E
    pair = 2 * HEAD_DIM
    n_pairs = (n_heads + N_KV_HEADS) // 2

    if rope:
        (ubuf,) = refs[3 + n_kv32:]
        ubuf[...] = u
        project = lambda p: _dot(ubuf[...], w_ref[:, p * pair:(p + 1) * pair])
        in_flight = 2
    else:
        qkv = _dot(u, w_ref[...])
        project = lambda p: qkv[:, p * pair:(p + 1) * pair]
        in_flight = n_pairs + 1

    def issue_after(done):
        tile = (slice(0, V7X_BF16_ROWS), slice(0, HEAD_DIM))
        blk = done[0:V7X_BF16_ROWS, :]
        ubuf[tile] = ubuf[tile] + (blk - blk).astype(BF16)

    ones = jnp.ones((HEAD_DIM, HEAD_DIM), BF16)

    def head(hv, gain):
        if rope:
            ms = _dot((hv * hv).astype(BF16), ones) * (1.0 / HEAD_DIM)
            hv = hv * lax.rsqrt(ms + EPS) * gain
            hv = (hv * cos_ref[...]
                  + pltpu.roll(hv, HEAD_DIM - HEAD_DIM // 4, 1) * sina_ref[...]
                  + pltpu.roll(hv, HEAD_DIM // 4, 1) * sinb_ref[...])
        else:
            hv = _rms(hv) * gain
        return hv

    raw = {p: project(p) for p in range(min(in_flight, n_pairs + 1))}
    for p in range(n_pairs):
        first = None
        for idx in (2 * p, 2 * p + 1):
            lanes = slice((idx % 2) * HEAD_DIM, (idx % 2 + 1) * HEAD_DIM)
            if idx < n_heads:
                hv = head(raw[p][:, lanes], qg_ref[...])
                q_ref[:, idx * HEAD_DIM:(idx + 1) * HEAD_DIM] = (hv * q_scale).astype(BF16)
            else:
                hk = idx - n_heads
                hv = head(raw[p][:, lanes], kg_ref[...])
                k_ref[:, hk * HEAD_DIM:(hk + 1) * HEAD_DIM] = hv.astype(BF16)
                if emit_f32_kv:
                    kv32_refs[0][pl.ds(hk, t, stride=N_KV_HEADS), :] = hv
            first = hv if first is None else first
        del raw[p]
        if p + in_flight <= n_pairs:
            issue_after(first)
            raw[p + in_flight] = project(p + in_flight)
    v = raw[n_pairs]
    for hk in range(N_KV_HEADS):
        lanes = slice(hk * HEAD_DIM, (hk + 1) * HEAD_DIM)
        v_ref[:, lanes] = v[:, lanes].astype(BF16)
        if emit_f32_kv:
            kv32_refs[1][pl.ds(hk, t, stride=N_KV_HEADS), :] = v[:, lanes]


def _qkv(name, layer, x, mod, gpre, w_qkv, q_gain, k_gain, seq_len, t, rope_tables=None, emit_f32_kv=False,
         casts=()):
    rows, d = x.shape
    n_heads = d // HEAD_DIM
    kv_w = N_KV_HEADS * HEAD_DIM
    tiles_per_seq = max(seq_len // t, 1)
    per_seq_mod = mod.shape[0] > 1
    rope = rope_tables is not None
    assert not (per_seq_mod or rope) or seq_len % t == 0

    def mod_map(i):
        return ((i // tiles_per_seq) if per_seq_mod else 0, 0, 0)

    in_specs = [
        pl.BlockSpec((t, d), lambda i: (i, 0)),
        pl.BlockSpec((1, 1, mod.shape[2]), mod_map),
        _layer_spec(gpre, layer[0]), _layer_spec(w_qkv, layer[1]),
        _layer_spec(q_gain, layer[1]), _layer_spec(k_gain, layer[1]),
    ]
    args = [x, mod, gpre, w_qkv, q_gain, k_gain]
    if rope:
        in_specs += [pl.BlockSpec((t, HEAD_DIM), lambda i: (i % tiles_per_seq, 0))] * 3
        args += list(rope_tables)
    out_specs = [pl.BlockSpec((t, d), lambda i: (i, 0)),
                 pl.BlockSpec((t, kv_w), lambda i: (i, 0)),
                 pl.BlockSpec((t, kv_w), lambda i: (i, 0))]
    out_shape = [jax.ShapeDtypeStruct((rows, d), BF16),
                 jax.ShapeDtypeStruct((rows, kv_w), BF16),
                 jax.ShapeDtypeStruct((rows, kv_w), BF16)]
    if emit_f32_kv:
        out_specs += [pl.BlockSpec((N_KV_HEADS * t, HEAD_DIM), lambda i: (i, 0))] * 2
        out_shape += [jax.ShapeDtypeStruct((N_KV_HEADS * rows, HEAD_DIM), F32)] * 2
    cast_in, cast_out, cast_shape = _cast_specs(casts, rows // t)
    body = functools.partial(_qkv_kernel, d=d, n_heads=n_heads, rope=rope, emit_f32_kv=emit_f32_kv)
    out = pl.pallas_call(
        _with_side_casts(body, len(in_specs), len(out_specs), len(casts)),
        grid=(rows // t,),
        in_specs=in_specs + cast_in,
        out_specs=out_specs + cast_out,
        out_shape=out_shape + cast_shape,
        scratch_shapes=[pltpu.VMEM((t, d), BF16)] if rope else [],
        compiler_params=pltpu.CompilerParams(dimension_semantics=("arbitrary",),
                                             vmem_limit_bytes=V7X_VMEM_LIMIT_BYTES),
        name=name,
    )(*args, *[a for a, _ in casts])
    return out[:len(out_specs)], out[len(out_specs):]


def _kv_block(ref, kk, b0, kb):
    if ref.shape[1] == HEAD_DIM:
        return ref[pl.ds(N_KV_HEADS * b0 + kk, kb, stride=N_KV_HEADS), :].astype(BF16)
    return ref[b0:b0 + kb, kk * HEAD_DIM:(kk + 1) * HEAD_DIM]


def _attn_kernel(*refs, d, n_heads, n_kv_sets, seqs_per_tile):
    x_ref, q_ref = refs[:2]
    kv_refs = refs[2:2 + 2 * n_kv_sets]
    mod_ref, gpost_ref, wo_ref, o_ref, obuf = refs[2 + 2 * n_kv_sets:]
    tq = q_ref.shape[0] // seqs_per_tile
    group = n_heads // N_KV_HEADS
    for sub, kk in [(a, b) for a in range(seqs_per_tile) for b in range(N_KV_HEADS)]:
        rows = slice(sub * tq, (sub + 1) * tq)
        qs = jnp.concatenate(
            [q_ref[rows, (kk * group + g) * HEAD_DIM:(kk * group + g + 1) * HEAD_DIM] for g in range(group)], axis=0)
        m = denom = acc = None
        for s in range(n_kv_sets):
            k_ref, v_ref = kv_refs[2 * s], kv_refs[2 * s + 1]
            n_keys = k_ref.shape[0] * k_ref.shape[1] // (N_KV_HEADS * HEAD_DIM * seqs_per_tile)
            kb = min(KEY_BLOCK, n_keys)
            for b0 in range(sub * n_keys, (sub + 1) * n_keys, kb):
                sc = _dot_nt(qs, _kv_block(k_ref, kk, b0, kb))
                v_blk = _kv_block(v_ref, kk, b0, kb)
                mb = sc.max(axis=-1, keepdims=True)
                if m is None:
                    m = mb
                    p = jnp.exp2(sc - m)
                    denom = p.sum(axis=-1, keepdims=True)
                    acc = _dot(p.astype(BF16), v_blk)
                else:
                    m_new = jnp.maximum(m, mb)
                    alpha = jnp.exp2(m - m_new)
                    p = jnp.exp2(sc - m_new)
                    denom = alpha * denom + p.sum(axis=-1, keepdims=True)
                    acc = alpha * acc + _dot(p.astype(BF16), v_blk)
                    m = m_new
        out = acc * (1.0 / denom)
        for g in range(group):
            hq = kk * group + g
            obuf[rows, hq * HEAD_DIM:(hq + 1) * HEAD_DIM] = out[g * tq:(g + 1) * tq].astype(BF16)
    y = _dot(obuf[...], wo_ref[...])
    _finish(x_ref, y, mod_ref, gpost_ref, o_ref, d, 2)


def _attention(name, layer, x, q, kv_sets, mod, gpost, w_o, seq_len, tq, casts=()):
    rows, d = x.shape
    n_heads = d // HEAD_DIM
    kv_w = N_KV_HEADS * HEAD_DIM
    assert seq_len % tq == 0 or tq % seq_len == 0
    tiles_per_seq = max(seq_len // tq, 1)
    seqs_per_tile = max(tq // seq_len, 1)
    per_seq_mod = mod.shape[0] > 1
    assert not per_seq_mod or seqs_per_tile == 1

    def mod_map(i):
        return ((i // tiles_per_seq) if per_seq_mod else 0, 0, 0)

    in_specs = [pl.BlockSpec((tq, d), lambda i: (i, 0)), pl.BlockSpec((tq, d), lambda i: (i, 0))]
    args = [x, q]
    for k, v, n in kv_sets:
        block = (seqs_per_tile * n * kv_w // k.shape[1], k.shape[1])
        in_specs += [pl.BlockSpec(block, lambda i: (i // tiles_per_seq, 0))] * 2
        args += [k, v]
    in_specs += [pl.BlockSpec((1, 1, mod.shape[2]), mod_map),
                 _layer_spec(gpost, layer[0]), _layer_spec(w_o, layer[1])]
    args += [mod, gpost, w_o]
    cast_in, cast_out, cast_shape = _cast_specs(casts, rows // tq)
    body = functools.partial(_attn_kernel, d=d, n_heads=n_heads, n_kv_sets=len(kv_sets), seqs_per_tile=seqs_per_tile)
    out = pl.pallas_call(
        _with_side_casts(body, len(in_specs), 1, len(casts)),
        grid=(rows // tq,),
        in_specs=in_specs + cast_in,
        out_specs=[pl.BlockSpec((tq, d), lambda i: (i, 0))] + cast_out,
        out_shape=[jax.ShapeDtypeStruct((rows, d), F32)] + cast_shape,
        scratch_shapes=[pltpu.VMEM((tq, d), BF16)],
        compiler_params=pltpu.CompilerParams(dimension_semantics=("arbitrary",),
                                             vmem_limit_bytes=V7X_VMEM_LIMIT_BYTES),
        name=name,
    )(*args, *[a for a, _ in casts])
    return out[0], out[1:]


def _rope_tables(n_lat):
    half = HEAD_DIM // 2
    rows_n = n_lat // GRID_W
    row = np.repeat(np.arange(rows_n), GRID_W).astype(np.float32)
    col = np.tile(np.arange(GRID_W), rows_n).astype(np.float32)
    inv = (np.float32(ROPE_THETA) ** (-np.arange(0, half, 2, dtype=np.float32) / np.float32(half))).astype(np.float32)
    ang_r = row[:, None] * inv[None, :]
    ang_c = col[:, None] * inv[None, :]
    cr, sr, cc, sc = np.cos(ang_r), np.sin(ang_r), np.cos(ang_c), np.sin(ang_c)
    zero = np.zeros_like(sr)
    cos = np.concatenate([cr, cr, cc, cc], axis=-1)
    sin_a = np.concatenate([-sr, zero, -sc, zero], axis=-1)
    sin_b = np.concatenate([zero, sr, zero, sc], axis=-1)
    return jnp.asarray(cos), jnp.asarray(sin_a), jnp.asarray(sin_b)


def kernel(x_prompt, x_sample, cache_k, cache_v, c, c_ctx, mod_w, mod_b, norm_mix_pre, norm_mix_post, norm_ffn_pre, norm_ffn_post, conv_w_in, conv_k, conv_w_out, attn_w_qkv, attn_q_gain, attn_k_gain, attn_w_o, ffn_w_up, ffn_conv, ffn_w_down):
    batch, seq, d = x_prompt.shape
    dec_batch, dec_seq, _ = x_sample.shape
    depth = mod_w.shape[0]
    past_len = cache_k.shape[2]
    t_ctx = 4 * seq
    tq_ctx = 2 * seq
    t_lat = 1024
    tq_lat = 256

    n_cond = dec_batch + 1
    pad = (-n_cond) % V7X_BF16_ROWS
    cvec = jnp.concatenate([c, c_ctx[None, :], jnp.zeros((pad, d), F32)], axis=0)
    mod_all = _modulation(cvec, mod_w, mod_b)

    per_layer_rows = lambda a: a.reshape(a.shape[0], 1, a.shape[1])
    g_mix_pre, g_mix_post = per_layer_rows(norm_mix_pre), per_layer_rows(norm_mix_post)
    g_ffn_pre, g_ffn_post = per_layer_rows(norm_ffn_pre), per_layer_rows(norm_ffn_post)
    q_gain, k_gain = per_layer_rows(attn_q_gain), per_layer_rows(attn_k_gain)
    w_in, w_out = conv_w_in.astype(BF16), conv_w_out.astype(BF16)
    w_qkv, w_o = attn_w_qkv.astype(BF16), attn_w_o.astype(BF16)

    h_ctx = x_prompt.reshape(batch * seq, d)
    h_lat = x_sample.reshape(dec_batch * dec_seq, d)
    rope_tables = _rope_tables(dec_seq)
    new_k, new_v = [], []

    for i in range(depth):
        j = i // 2
        mod_lat = mod_all[i, :dec_batch][:, None, :]
        mod_ctx = mod_all[i, dec_batch:dec_batch + 1][:, None, :]
        if i % 2 == 0:
            h_ctx, (w_down,) = _mixer(f"mixer{i}_ctx", (i, j, j), h_ctx, mod_ctx, g_mix_pre, g_mix_post, w_in, conv_k,
                                      w_out, seq, t_ctx, casts=[(ffn_w_down, i)])
            h_lat, (w_up,) = _mixer(f"mixer{i}_lat", (i, j, j), h_lat, mod_lat, g_mix_pre, g_mix_post, w_in, conv_k,
                                    w_out, dec_seq, t_lat, casts=[(ffn_w_up, i)])
        else:
            (q_c, k_c, v_c, k_c32, v_c32), (w_down,) = _qkv(
                f"qkv{i}_ctx", (i, j), h_ctx, mod_ctx, g_mix_pre, w_qkv, q_gain, k_gain, seq, t_ctx,
                emit_f32_kv=True, casts=[(ffn_w_down, i)])
            new_k.append(k_c32.reshape(batch, seq, N_KV_HEADS, HEAD_DIM))
            new_v.append(v_c32.reshape(batch, seq, N_KV_HEADS, HEAD_DIM))
            h_ctx, (w_up,) = _attention(f"attn{i}_ctx", (i, j), h_ctx, q_c, [(k_c, v_c, seq)], mod_ctx, g_mix_post, w_o,
                                        seq, tq_ctx, casts=[(ffn_w_up, i)])
            (q_l, k_l, v_l), _ = _qkv(f"qkv{i}_lat", (i, j), h_lat, mod_lat, g_mix_pre, w_qkv, q_gain, k_gain,
                                      dec_seq, t_lat, rope_tables=rope_tables)
            ck = cache_k[:, j].reshape(dec_batch * past_len * N_KV_HEADS, HEAD_DIM)
            cv = cache_v[:, j].reshape(dec_batch * past_len * N_KV_HEADS, HEAD_DIM)
            h_lat, _ = _attention(f"attn{i}_lat", (i, j), h_lat, q_l, [(ck, cv, past_len), (k_l, v_l, dec_seq)],
                                  mod_lat, g_mix_post, w_o, dec_seq, tq_lat)
        h_ctx = _ffn(f"ffn{i}_ctx", (i, 0, i), h_ctx, mod_ctx, g_ffn_pre, g_ffn_post, w_up[None], ffn_conv, w_down[None],
                     seq, t_ctx)
        h_lat = _ffn(f"ffn{i}_lat", (i, 0, i), h_lat, mod_lat, g_ffn_pre, g_ffn_post, w_up[None], ffn_conv, w_down[None],
                     dec_seq, t_lat)

    return (h_ctx.reshape(batch, seq, d), h_lat.reshape(dec_batch, dec_seq, d),
            jnp.stack(new_k, axis=1), jnp.stack(new_v, axis=1))
```

```python
import functools
from typing import NamedTuple

import jax
import jax.numpy as jnp
import numpy as np
from jax import lax
from jax.experimental import pallas as pl
from jax.experimental.pallas import tpu as pltpu

EPS = 1e-6
ROPE_THETA = 10000.0
GRID_W = 64
HEAD_DIM = 128
N_KV_HEADS = 2

V7X_SUBLANES = 8
V7X_BF16_ROWS = 16
V7X_VMEM_LIMIT_BYTES = 56 * 1024 * 1024

HALO = V7X_SUBLANES
CONV_CHUNK = 256
KEY_BLOCK = 1024
MOD_COLS = 1536
LOG2_E = 1.4426950408889634

BF16 = jnp.bfloat16
F32 = jnp.float32


def _resident(block_shape, index_map):
    return pl.BlockSpec(block_shape, index_map, pipeline_mode=pl.Buffered(1))


def _layer_spec(stacked, layer):
    zeros = (0,) * (stacked.ndim - 1)
    return _resident((None,) + stacked.shape[1:], lambda i: (layer,) + zeros)


class _Mod(NamedTuple):
    table: jax.Array
    layer: int
    row: int
    per_seq: bool


def _mod_spec(mod, tiles_per_seq):
    def index(i):
        return (mod.layer, mod.row + ((i // tiles_per_seq) if mod.per_seq else 0), 0, 0)
    return pl.BlockSpec((None, 1, 1, mod.table.shape[3]), index)


def _with_side_casts(kernel_fn, n_in, n_out, n_casts):
    if n_casts == 0:
        return kernel_fn

    def body(*refs):
        ins, rest = refs[:n_in], refs[n_in:]
        srcs, rest = rest[:n_casts], rest[n_casts:]
        outs, rest = rest[:n_out], rest[n_out:]
        dsts, scratch = rest[:n_casts], rest[n_casts:]
        for src, dst in zip(srcs, dsts):
            dst[...] = src[...].astype(BF16)
        kernel_fn(*ins, *outs, *scratch)

    return body


def _cast_specs(casts, steps):
    in_specs, out_specs, out_shape = [], [], []
    for stacked, layer in casts:
        _, rows, cols = stacked.shape
        assert rows % (steps * V7X_BF16_ROWS) == 0
        blk = rows // steps
        in_specs.append(pl.BlockSpec((None, blk, cols), lambda i, layer=layer: (layer, i, 0)))
        out_specs.append(pl.BlockSpec((blk, cols), lambda i: (i, 0)))
        out_shape.append(jax.ShapeDtypeStruct((rows, cols), BF16))
    return in_specs, out_specs, out_shape


def _rms(x):
    return x * lax.rsqrt(jnp.mean(x * x, axis=-1, keepdims=True) + EPS)


def _dot(a, b):
    return jnp.dot(a, b, preferred_element_type=F32)


def _dot_nt(a, b):
    return lax.dot_general(a, b, (((1,), (1,)), ((), ())), preferred_element_type=F32)


def _mod_kernel(c_ref, w_ref, b_ref, o_ref):
    c = c_ref[...]
    s = c * (1.0 / (1.0 + jnp.exp(-c)))
    o_ref[0] = _dot(s.astype(BF16), w_ref[0].astype(BF16)) + b_ref[0]


def _modulation(cvec, mod_w, mod_b):
    depth, d, n = mod_w.shape
    rows = cvec.shape[0]
    tn = MOD_COLS
    assert n % tn == 0
    return pl.pallas_call(
        _mod_kernel,
        grid=(depth, n // tn),
        in_specs=[
            pl.BlockSpec((rows, d), lambda l, j: (0, 0)),
            pl.BlockSpec((1, d, tn), lambda l, j: (l, 0, j)),
            pl.BlockSpec((1, 1, tn), lambda l, j: (l, 0, j)),
        ],
        out_specs=pl.BlockSpec((1, rows, tn), lambda l, j: (l, 0, j)),
        out_shape=jax.ShapeDtypeStruct((depth, rows, n), F32),
        compiler_params=pltpu.CompilerParams(dimension_semantics=("arbitrary", "arbitrary")),
        name="modulation",
    )(cvec, mod_w, mod_b.reshape(depth, 1, n))


def _fill_u(refs, has_halo, mod_ref, gpre_ref, ubuf, t, d, mod_off):
    shift = mod_ref[0, :, mod_off * d:(mod_off + 1) * d]
    scale = mod_ref[0, :, (mod_off + 1) * d:(mod_off + 2) * d]
    gs = gpre_ref[...] * (1.0 + scale)
    if has_halo:
        xp_ref, x_ref, xn_ref = refs
        halo = jnp.concatenate([xp_ref[...], xn_ref[...]], axis=0)
        ubuf[t:t + 2 * HALO, :] = (_rms(halo) * gs + shift).astype(BF16)
    else:
        (x_ref,) = refs
    ubuf[0:t, :] = (_rms(x_ref[...]) * gs + shift).astype(BF16)
    return x_ref


def _edge_flags(tiles_per_seq):
    i = pl.program_id(0)
    pos = lax.rem(i, tiles_per_seq)
    return (pos != 0).astype(F32), (pos != tiles_per_seq - 1).astype(F32)


def _row_masks(t, rows_per_seq):
    pos = lax.rem(lax.broadcasted_iota(jnp.int32, (t, CONV_CHUNK), 0), rows_per_seq)
    return pos == 0, pos == rows_per_seq - 1


def _conv3(h, k, t, has_halo, flags, first_row, last_row):
    zm = h[0:t]
    if has_halo:
        zp = h[t + HALO - 1:t + HALO] * flags[0]
        zn = h[t + HALO:t + HALO + 1] * flags[1]
    else:
        zp = jnp.zeros((1, h.shape[1]), F32)
        zn = zp
    z_prev = jnp.where(first_row, zp, pltpu.roll(zm, 1, 0))
    z_next = jnp.where(last_row, zn, pltpu.roll(zm, t - 1, 0))
    return z_prev * k[0:1] + zm * k[1:2] + z_next * k[2:3]


def _finish(x_ref, y, mod_ref, gpost_ref, o_ref, d, gate_off):
    gate = mod_ref[0, :, gate_off * d:(gate_off + 1) * d]
    o_ref[...] = x_ref[...] + (gate * gpost_ref[...]) * _rms(y)


def _mixer_kernel(*refs, t, d, has_halo, tiles_per_seq, rows_per_seq):
    n_x = 3 if has_halo else 1
    x_refs = refs[:n_x]
    mod_ref, gpre_ref, gpost_ref, win_ref, ck_ref, wout_ref, o_ref, ubuf, ybuf = refs[n_x:]
    x_ref = _fill_u(x_refs, has_halo, mod_ref, gpre_ref, ubuf, t, d, 0)
    flags = _edge_flags(tiles_per_seq) if has_halo else None
    first_row, last_row = _row_masks(t, rows_per_seq)
    for j in range(d // CONV_CHUNK):
        c0 = j * CONV_CHUNK
        b_gate = _dot(ubuf[0:t, :], win_ref[:, c0:c0 + CONV_CHUNK])
        c_gate = _dot(ubuf[...], win_ref[:, d + c0:d + c0 + CONV_CHUNK])
        xp = _dot(ubuf[...], win_ref[:, 2 * d + c0:2 * d + c0 + CONV_CHUNK])
        conv = _conv3(c_gate * xp, ck_ref[:, c0:c0 + CONV_CHUNK], t, has_halo, flags, first_row, last_row)
        ybuf[:, c0:c0 + CONV_CHUNK] = (b_gate * conv).astype(BF16)
    y = _dot(ybuf[...], wout_ref[...])
    _finish(x_ref, y, mod_ref, gpost_ref, o_ref, d, 2)


def _ffn_kernel(*refs, t, d, d_ff, has_halo, tiles_per_seq, rows_per_seq):
    n_x = 3 if has_halo else 1
    x_refs = refs[:n_x]
    mod_ref, gpre_ref, gpost_ref, wup_ref, ck_ref, wdown_ref, o_ref, ubuf, abuf = refs[n_x:]
    x_ref = _fill_u(x_refs, has_halo, mod_ref, gpre_ref, ubuf, t, d, 3)
    flags = _edge_flags(tiles_per_seq) if has_halo else None
    first_row, last_row = _row_masks(t, rows_per_seq)
    for j in range(d_ff // CONV_CHUNK):
        c0 = j * CONV_CHUNK
        hg = _dot(ubuf[...], wup_ref[:, c0:c0 + CONV_CHUNK])
        hu = _dot(ubuf[...], wup_ref[:, d_ff + c0:d_ff + c0 + CONV_CHUNK])
        g = _conv3(hg, ck_ref[:, c0:c0 + CONV_CHUNK], t, has_halo, flags, first_row, last_row)
        u = _conv3(hu, ck_ref[:, d_ff + c0:d_ff + c0 + CONV_CHUNK], t, has_halo, flags, first_row, last_row)
        act = g * (1.0 / (1.0 + jnp.exp(-g))) * u
        abuf[:, c0:c0 + CONV_CHUNK] = act.astype(BF16)
    y = _dot(abuf[...], wdown_ref[...])
    _finish(x_ref, y, mod_ref, gpost_ref, o_ref, d, 5)


def _conv_sublayer(kernel_fn, name, layer, x, mod, gpre, gpost, w_a, ck, w_b, seq_len, t, casts=()):
    rows, d = x.shape
    assert rows % t == 0 and t % V7X_BF16_ROWS == 0 and (seq_len % t == 0 or t % seq_len == 0)
    tiles_per_seq = max(seq_len // t, 1)
    has_halo = t < seq_len
    assert not mod.per_seq or t <= seq_len
    hb = t // HALO
    last_hb = rows // HALO - 1

    x_specs = [pl.BlockSpec((t, d), lambda i: (i, 0))]
    x_args = [x]
    if has_halo:
        x_specs = [pl.BlockSpec((HALO, d), lambda i: (jnp.maximum(i * hb - 1, 0), 0)),
                   x_specs[0],
                   pl.BlockSpec((HALO, d), lambda i: (jnp.minimum((i + 1) * hb, last_hb), 0))]
        x_args = [x, x, x]
    m_rows = t + 2 * HALO if has_halo else t
    body = functools.partial(kernel_fn, t=t, d=d, has_halo=has_halo, tiles_per_seq=tiles_per_seq,
                             rows_per_seq=min(t, seq_len))
    in_specs = x_specs + [
        _mod_spec(mod, tiles_per_seq),
        _layer_spec(gpre, layer[0]), _layer_spec(gpost, layer[0]),
        _layer_spec(w_a, layer[1]), _layer_spec(ck, layer[2]), _layer_spec(w_b, layer[1]),
    ]
    cast_in, cast_out, cast_shape = _cast_specs(casts, rows // t)
    out = pl.pallas_call(
        _with_side_casts(body, len(in_specs), 1, len(casts)),
        grid=(rows // t,),
        in_specs=in_specs + cast_in,
        out_specs=[pl.BlockSpec((t, d), lambda i: (i, 0))] + cast_out,
        out_shape=[jax.ShapeDtypeStruct((rows, d), F32)] + cast_shape,
        scratch_shapes=[pltpu.VMEM((m_rows, d), BF16), pltpu.VMEM((t, w_b.shape[1]), BF16)],
        compiler_params=pltpu.CompilerParams(dimension_semantics=("arbitrary",),
                                             vmem_limit_bytes=V7X_VMEM_LIMIT_BYTES),
        name=name,
    )(*x_args, mod.table, gpre, gpost, w_a, ck, w_b, *[a for a, _ in casts])
    return out[0], out[1:]


def _mixer(name, layer, x, mod, gpre, gpost, w_in, ck, w_out, seq_len, t, casts=()):
    return _conv_sublayer(_mixer_kernel, name, layer, x, mod, gpre, gpost, w_in, ck, w_out, seq_len, t, casts)


def _ffn(name, layer, x, mod, gpre, gpost, w_up, ck, w_down, seq_len, t, casts=()):
    kernel_fn = functools.partial(_ffn_kernel, d_ff=w_down.shape[1])
    return _conv_sublayer(kernel_fn, name, layer, x, mod, gpre, gpost, w_up, ck, w_down, seq_len, t, casts)


def _qkv_kernel(*refs, d, n_heads, rope, emit_f32_kv):
    x_ref, mod_ref, gpre_ref, w_ref, qg_ref, kg_ref = refs[:6]
    refs = refs[6:]
    if rope:
        cos_ref, sina_ref, sinb_ref = refs[:3]
        refs = refs[3:]
    q_ref, k_ref, v_ref = refs[:3]
    n_kv32 = 2 if emit_f32_kv else 0
    kv32_refs = refs[3:3 + n_kv32]
    t = x_ref.shape[0]
    shift = mod_ref[0, :, 0:d]
    scale = mod_ref[0, :, d:2 * d]
    u = (_rms(x_ref[...]) * (gpre_ref[...] * (1.0 + scale)) + shift).astype(BF16)
    q_scale = HEAD_DIM ** -0.5 * LOG2_E
    pair = 2 * HEAD_DIM
    n_pairs = (n_heads + N_KV_HEADS) // 2

    if rope:
        (ubuf,) = refs[3 + n_kv32:]
        ubuf[...] = u
        project = lambda p: _dot(ubuf[...], w_ref[:, p * pair:(p + 1) * pair])
        in_flight = 2
    else:
        qkv = _dot(u, w_ref[...])
        project = lambda p: qkv[:, p * pair:(p + 1) * pair]
        in_flight = n_pairs + 1

    def issue_after(done):
        tile = (slice(0, V7X_BF16_ROWS), slice(0, HEAD_DIM))
        blk = done[0:V7X_BF16_ROWS, :]
        ubuf[tile] = ubuf[tile] + (blk - blk).astype(BF16)

    ones = jnp.ones((HEAD_DIM, HEAD_DIM), BF16)

    def head(hv, gain):
        if rope:
            ms = _dot((hv * hv).astype(BF16), ones) * (1.0 / HEAD_DIM)
            hv = hv * lax.rsqrt(ms + EPS) * gain
            hv = (hv * cos_ref[...]
                  + pltpu.roll(hv, HEAD_DIM - HEAD_DIM // 4, 1) * sina_ref[...]
                  + pltpu.roll(hv, HEAD_DIM // 4, 1) * sinb_ref[...])
        else:
            hv = _rms(hv) * gain
        return hv

    raw = {p: project(p) for p in range(min(in_flight, n_pairs + 1))}
    for p in range(n_pairs):
        first = None
        for idx in (2 * p, 2 * p + 1):
            lanes = slice((idx % 2) * HEAD_DIM, (idx % 2 + 1) * HEAD_DIM)
            if idx < n_heads:
                hv = head(raw[p][:, lanes], qg_ref[...])
                q_ref[:, idx * HEAD_DIM:(idx + 1) * HEAD_DIM] = (hv * q_scale).astype(BF16)
            else:
                hk = idx - n_heads
                hv = head(raw[p][:, lanes], kg_ref[...])
                k_ref[:, hk * HEAD_DIM:(hk + 1) * HEAD_DIM] = hv.astype(BF16)
                if emit_f32_kv:
                    kv32_refs[0][pl.ds(hk, t, stride=N_KV_HEADS), :] = hv
            first = hv if first is None else first
        del raw[p]
        if p + in_flight <= n_pairs:
            issue_after(first)
            raw[p + in_flight] = project(p + in_flight)
    v = raw[n_pairs]
    for hk in range(N_KV_HEADS):
        lanes = slice(hk * HEAD_DIM, (hk + 1) * HEAD_DIM)
        v_ref[:, lanes] = v[:, lanes].astype(BF16)
        if emit_f32_kv:
            kv32_refs[1][pl.ds(hk, t, stride=N_KV_HEADS), :] = v[:, lanes]


def _qkv(name, layer, x, mod, gpre, w_qkv, q_gain, k_gain, seq_len, t, rope_tables=None, emit_f32_kv=False,
         casts=()):
    rows, d = x.shape
    n_heads = d // HEAD_DIM
    kv_w = N_KV_HEADS * HEAD_DIM
    tiles_per_seq = max(seq_len // t, 1)
    rope = rope_tables is not None
    assert not (mod.per_seq or rope) or seq_len % t == 0
    in_specs = [
        pl.BlockSpec((t, d), lambda i: (i, 0)),
        _mod_spec(mod, tiles_per_seq),
        _layer_spec(gpre, layer[0]), _layer_spec(w_qkv, layer[1]),
        _layer_spec(q_gain, layer[2]), _layer_spec(k_gain, layer[2]),
    ]
    args = [x, mod.table, gpre, w_qkv, q_gain, k_gain]
    if rope:
        in_specs += [pl.BlockSpec((t, HEAD_DIM), lambda i: (i % tiles_per_seq, 0))] * 3
        args += list(rope_tables)
    out_specs = [pl.BlockSpec((t, d), lambda i: (i, 0)),
                 pl.BlockSpec((t, kv_w), lambda i: (i, 0)),
                 pl.BlockSpec((t, kv_w), lambda i: (i, 0))]
    out_shape = [jax.ShapeDtypeStruct((rows, d), BF16),
                 jax.ShapeDtypeStruct((rows, kv_w), BF16),
                 jax.ShapeDtypeStruct((rows, kv_w), BF16)]
    if emit_f32_kv:
        out_specs += [pl.BlockSpec((N_KV_HEADS * t, HEAD_DIM), lambda i: (i, 0))] * 2
        out_shape += [jax.ShapeDtypeStruct((N_KV_HEADS * rows, HEAD_DIM), F32)] * 2
    cast_in, cast_out, cast_shape = _cast_specs(casts, rows // t)
    body = functools.partial(_qkv_kernel, d=d, n_heads=n_heads, rope=rope, emit_f32_kv=emit_f32_kv)
    out = pl.pallas_call(
        _with_side_casts(body, len(in_specs), len(out_specs), len(casts)),
        grid=(rows // t,),
        in_specs=in_specs + cast_in,
        out_specs=out_specs + cast_out,
        out_shape=out_shape + cast_shape,
        scratch_shapes=[pltpu.VMEM((t, d), BF16)] if rope else [],
        compiler_params=pltpu.CompilerParams(dimension_semantics=("arbitrary",),
                                             vmem_limit_bytes=V7X_VMEM_LIMIT_BYTES),
        name=name,
    )(*args, *[a for a, _ in casts])
    return out[:len(out_specs)], out[len(out_specs):]


def _kv_block(ref, kk, b0, kb):
    if ref.shape[1] == HEAD_DIM:
        return ref[pl.ds(N_KV_HEADS * b0 + kk, kb, stride=N_KV_HEADS), :].astype(BF16)
    return ref[b0:b0 + kb, kk * HEAD_DIM:(kk + 1) * HEAD_DIM]


def _attn_kernel(*refs, d, n_heads, n_kv_sets, seqs_per_tile):
    x_ref, q_ref = refs[:2]
    kv_refs = refs[2:2 + 2 * n_kv_sets]
    mod_ref, gpost_ref, wo_ref, o_ref, obuf = refs[2 + 2 * n_kv_sets:]
    tq = q_ref.shape[0] // seqs_per_tile
    group = n_heads // N_KV_HEADS
    for sub, kk in [(a, b) for a in range(seqs_per_tile) for b in range(N_KV_HEADS)]:
        rows = slice(sub * tq, (sub + 1) * tq)
        qs = jnp.concatenate(
            [q_ref[rows, (kk * group + g) * HEAD_DIM:(kk * group + g + 1) * HEAD_DIM] for g in range(group)], axis=0)
        m = denom = acc = None
        for s in range(n_kv_sets):
            k_ref, v_ref = kv_refs[2 * s], kv_refs[2 * s + 1]
            n_keys = k_ref.shape[0] * k_ref.shape[1] // (N_KV_HEADS * HEAD_DIM * seqs_per_tile)
            kb = min(KEY_BLOCK, n_keys)
            for b0 in range(sub * n_keys, (sub + 1) * n_keys, kb):
                sc = _dot_nt(qs, _kv_block(k_ref, kk, b0, kb))
                v_blk = _kv_block(v_ref, kk, b0, kb)
                mb = sc.max(axis=-1, keepdims=True)
                if m is None:
                    m = mb
                    p = jnp.exp2(sc - m)
                    denom = p.sum(axis=-1, keepdims=True)
                    acc = _dot(p.astype(BF16), v_blk)
                else:
                    m_new = jnp.maximum(m, mb)
                    alpha = jnp.exp2(m - m_new)
                    p = jnp.exp2(sc - m_new)
                    denom = alpha * denom + p.sum(axis=-1, keepdims=True)
                    acc = alpha * acc + _dot(p.astype(BF16), v_blk)
                    m = m_new
        out = acc * (1.0 / denom)
        for g in range(group):
            hq = kk * group + g
            obuf[rows, hq * HEAD_DIM:(hq + 1) * HEAD_DIM] = out[g * tq:(g + 1) * tq].astype(BF16)
    y = _dot(obuf[...], wo_ref[...])
    _finish(x_ref, y, mod_ref, gpost_ref, o_ref, d, 2)


def _attention(name, layer, x, q, kv_sets, mod, gpost, w_o, seq_len, tq, casts=()):
    rows, d = x.shape
    n_heads = d // HEAD_DIM
    kv_w = N_KV_HEADS * HEAD_DIM
    assert seq_len % tq == 0 or tq % seq_len == 0
    tiles_per_seq = max(seq_len // tq, 1)
    seqs_per_tile = max(tq // seq_len, 1)
    assert not mod.per_seq or seqs_per_tile == 1
    in_specs = [pl.BlockSpec((tq, d), lambda i: (i, 0)), pl.BlockSpec((tq, d), lambda i: (i, 0))]
    args = [x, q]
    for k, v, n in kv_sets:
        block = (seqs_per_tile * n * kv_w // k.shape[1], k.shape[1])
        in_specs += [pl.BlockSpec(block, lambda i: (i // tiles_per_seq, 0))] * 2
        args += [k, v]
    in_specs += [_mod_spec(mod, tiles_per_seq),
                 _layer_spec(gpost, layer[0]), _layer_spec(w_o, layer[1])]
    args += [mod.table, gpost, w_o]
    cast_in, cast_out, cast_shape = _cast_specs(casts, rows // tq)
    body = functools.partial(_attn_kernel, d=d, n_heads=n_heads, n_kv_sets=len(kv_sets), seqs_per_tile=seqs_per_tile)
    out = pl.pallas_call(
        _with_side_casts(body, len(in_specs), 1, len(casts)),
        grid=(rows // tq,),
        in_specs=in_specs + cast_in,
        out_specs=[pl.BlockSpec((tq, d), lambda i: (i, 0))] + cast_out,
        out_shape=[jax.ShapeDtypeStruct((rows, d), F32)] + cast_shape,
        scratch_shapes=[pltpu.VMEM((tq, d), BF16)],
        compiler_params=pltpu.CompilerParams(dimension_semantics=("arbitrary",),
                                             vmem_limit_bytes=V7X_VMEM_LIMIT_BYTES),
        name=name,
    )(*args, *[a for a, _ in casts])
    return out[0], out[1:]


def _rope_tables(n_lat):
    half = HEAD_DIM // 2
    rows_n = n_lat // GRID_W
    row = np.repeat(np.arange(rows_n), GRID_W).astype(np.float32)
    col = np.tile(np.arange(GRID_W), rows_n).astype(np.float32)
    inv = (np.float32(ROPE_THETA) ** (-np.arange(0, half, 2, dtype=np.float32) / np.float32(half))).astype(np.float32)
    ang_r = row[:, None] * inv[None, :]
    ang_c = col[:, None] * inv[None, :]
    cr, sr, cc, sc = np.cos(ang_r), np.sin(ang_r), np.cos(ang_c), np.sin(ang_c)
    zero = np.zeros_like(sr)
    cos = np.concatenate([cr, cr, cc, cc], axis=-1)
    sin_a = np.concatenate([-sr, zero, -sc, zero], axis=-1)
    sin_b = np.concatenate([zero, sr, zero, sc], axis=-1)
    return jnp.asarray(cos), jnp.asarray(sin_a), jnp.asarray(sin_b)


def kernel(x_prompt, x_sample, cache_k, cache_v, c, c_ctx, mod_w, mod_b, norm_mix_pre, norm_mix_post, norm_ffn_pre, norm_ffn_post, conv_w_in, conv_k, conv_w_out, attn_w_qkv, attn_q_gain, attn_k_gain, attn_w_o, ffn_w_up, ffn_conv, ffn_w_down):
    batch, seq, d = x_prompt.shape
    dec_batch, dec_seq, _ = x_sample.shape
    depth = mod_w.shape[0]
    past_len = cache_k.shape[2]
    t_ctx = 4 * seq
    tq_ctx = 2 * seq
    t_lat = 1024
    tq_lat = 256

    n_cond = dec_batch + 1
    pad = (-n_cond) % V7X_BF16_ROWS
    cvec = jnp.concatenate([c, c_ctx[None, :], jnp.zeros((pad, d), F32)], axis=0)
    mod_all = _modulation(cvec, mod_w, mod_b)
    mod_table = mod_all.reshape(depth, cvec.shape[0], 1, mod_all.shape[2])

    per_layer_rows = lambda a: a.reshape(a.shape[0], 1, a.shape[1])
    g_mix_pre, g_mix_post = per_layer_rows(norm_mix_pre), per_layer_rows(norm_mix_post)
    g_ffn_pre, g_ffn_post = per_layer_rows(norm_ffn_pre), per_layer_rows(norm_ffn_post)
    q_gain, k_gain = per_layer_rows(attn_q_gain), per_layer_rows(attn_k_gain)
    w_in, w_out = conv_w_in.astype(BF16), conv_w_out.astype(BF16)
    attn_w = {}

    h_ctx = x_prompt.reshape(batch * seq, d)
    h_lat = x_sample.reshape(dec_batch * dec_seq, d)
    rope_tables = _rope_tables(dec_seq)
    new_k, new_v = [], []

    for i in range(depth):
        j = i // 2
        mod_lat = _Mod(mod_table, i, 0, True)
        mod_ctx = _Mod(mod_table, i, dec_batch, False)
        if i % 2 == 0:
            h_ctx, (w_down,) = _mixer(f"mixer{i}_ctx", (i, j, j), h_ctx, mod_ctx, g_mix_pre, g_mix_post, w_in, conv_k,
                                      w_out, seq, t_ctx, casts=[(ffn_w_down, i)])
            h_lat, (w_up,) = _mixer(f"mixer{i}_lat", (i, j, j), h_lat, mod_lat, g_mix_pre, g_mix_post, w_in, conv_k,
                                    w_out, dec_seq, t_lat, casts=[(ffn_w_up, i)])
        else:
            if i in attn_w:
                w_qkv, w_o = (w[None] for w in attn_w[i])
                lw = 0
            else:
                w_qkv, w_o, lw = attn_w_qkv.astype(BF16), attn_w_o.astype(BF16), j
            (q_c, k_c, v_c, k_c32, v_c32), (w_down,) = _qkv(
                f"qkv{i}_ctx", (i, lw, j), h_ctx, mod_ctx, g_mix_pre, w_qkv, q_gain, k_gain, seq, t_ctx,
                emit_f32_kv=True, casts=[(ffn_w_down, i)])
            new_k.append(k_c32.reshape(batch, seq, N_KV_HEADS, HEAD_DIM))
            new_v.append(v_c32.reshape(batch, seq, N_KV_HEADS, HEAD_DIM))
            h_ctx, (w_up,) = _attention(f"attn{i}_ctx", (i, lw), h_ctx, q_c, [(k_c, v_c, seq)], mod_ctx, g_mix_post, w_o,
                                        seq, tq_ctx, casts=[(ffn_w_up, i)])
            (q_l, k_l, v_l), _ = _qkv(f"qkv{i}_lat", (i, lw, j), h_lat, mod_lat, g_mix_pre, w_qkv, q_gain, k_gain,
                                      dec_seq, t_lat, rope_tables=rope_tables)
            ck = cache_k[:, j].reshape(dec_batch * past_len * N_KV_HEADS, HEAD_DIM)
            cv = cache_v[:, j].reshape(dec_batch * past_len * N_KV_HEADS, HEAD_DIM)
            h_lat, _ = _attention(f"attn{i}_lat", (i, lw), h_lat, q_l, [(ck, cv, past_len), (k_l, v_l, dec_seq)],
                                  mod_lat, g_mix_post, w_o, dec_seq, tq_lat)
        nxt = i + 1
        casts = [(attn_w_qkv, nxt // 2), (attn_w_o, nxt // 2)] if nxt < depth and nxt % 2 == 1 else []
        h_ctx, cast_out = _ffn(f"ffn{i}_ctx", (i, 0, i), h_ctx, mod_ctx, g_ffn_pre, g_ffn_post, w_up[None], ffn_conv,
                               w_down[None], seq, t_ctx, casts)
        if casts:
            attn_w[nxt] = cast_out
        h_lat, _ = _ffn(f"ffn{i}_lat", (i, 0, i), h_lat, mod_lat, g_ffn_pre, g_ffn_post, w_up[None], ffn_conv,
                        w_down[None], dec_seq, t_lat)

    return (h_ctx.reshape(batch, seq, d), h_lat.reshape(dec_batch, dec_seq, d),
            jnp.stack(new_k, axis=1), jnp.stack(new_v, axis=1))
```

```python
import functools

import jax
import jax.numpy as jnp
import numpy as np
from jax import lax
from jax.experimental import pallas as pl
from jax.experimental.pallas import tpu as pltpu

EPS = 1e-6
ROPE_THETA = 10000.0
GRID_W = 64
HEAD_DIM = 128
N_KV_HEADS = 2

V7X_SUBLANES = 8
V7X_BF16_ROWS = 16
V7X_VMEM_LIMIT_BYTES = 56 * 1024 * 1024

HALO = V7X_SUBLANES
CONV_CHUNK = 256
KEY_BLOCK = 1024
MOD_COLS = 1536
LOG2_E = 1.4426950408889634

BF16 = jnp.bfloat16
F32 = jnp.float32


def _resident(block_shape, index_map):
    return pl.BlockSpec(block_shape, index_map, pipeline_mode=pl.Buffered(1))


def _layer_spec(stacked, layer):
    zeros = (0,) * (stacked.ndim - 1)
    return _resident((None,) + stacked.shape[1:], lambda i: (layer,) + zeros)


def _with_side_casts(kernel_fn, n_in, n_out, n_casts):
    if n_casts == 0:
        return kernel_fn

    def body(*refs):
        ins, rest = refs[:n_in], refs[n_in:]
        srcs, rest = rest[:n_casts], rest[n_casts:]
        outs, rest = rest[:n_out], rest[n_out:]
        dsts, scratch = rest[:n_casts], rest[n_casts:]
        for src, dst in zip(srcs, dsts):
            dst[...] = src[...].astype(BF16)
        kernel_fn(*ins, *outs, *scratch)

    return body


def _cast_specs(casts, steps):
    in_specs, out_specs, out_shape = [], [], []
    for stacked, layer in casts:
        _, rows, cols = stacked.shape
        assert rows % (steps * V7X_BF16_ROWS) == 0
        blk = rows // steps
        in_specs.append(pl.BlockSpec((None, blk, cols), lambda i, layer=layer: (layer, i, 0)))
        out_specs.append(pl.BlockSpec((blk, cols), lambda i: (i, 0)))
        out_shape.append(jax.ShapeDtypeStruct((rows, cols), BF16))
    return in_specs, out_specs, out_shape


def _rms(x):
    return x * lax.rsqrt(jnp.mean(x * x, axis=-1, keepdims=True) + EPS)


def _dot(a, b):
    return jnp.dot(a, b, preferred_element_type=F32)


def _dot_nt(a, b):
    return lax.dot_general(a, b, (((1,), (1,)), ((), ())), preferred_element_type=F32)


def _mod_kernel(c_ref, w_ref, b_ref, o_ref):
    c = c_ref[...]
    s = c * (1.0 / (1.0 + jnp.exp(-c)))
    o_ref[0] = _dot(s.astype(BF16), w_ref[0].astype(BF16)) + b_ref[0]


def _modulation(cvec, mod_w, mod_b):
    depth, d, n = mod_w.shape
    rows = cvec.shape[0]
    tn = MOD_COLS
    assert n % tn == 0
    return pl.pallas_call(
        _mod_kernel,
        grid=(depth, n // tn),
        in_specs=[
            pl.BlockSpec((rows, d), lambda l, j: (0, 0)),
            pl.BlockSpec((1, d, tn), lambda l, j: (l, 0, j)),
            pl.BlockSpec((1, 1, tn), lambda l, j: (l, 0, j)),
        ],
        out_specs=pl.BlockSpec((1, rows, tn), lambda l, j: (l, 0, j)),
        out_shape=jax.ShapeDtypeStruct((depth, rows, n), F32),
        compiler_params=pltpu.CompilerParams(dimension_semantics=("arbitrary", "arbitrary")),
        name="modulation",
    )(cvec, mod_w, mod_b.reshape(depth, 1, n))


def _fill_u(refs, has_halo, mod_ref, gpre_ref, ubuf, t, d, mod_off):
    shift = mod_ref[0, :, mod_off * d:(mod_off + 1) * d]
    scale = mod_ref[0, :, (mod_off + 1) * d:(mod_off + 2) * d]
    gs = gpre_ref[...] * (1.0 + scale)
    if has_halo:
        xp_ref, x_ref, xn_ref = refs
        halo = jnp.concatenate([xp_ref[...], xn_ref[...]], axis=0)
        ubuf[t:t + 2 * HALO, :] = (_rms(halo) * gs + shift).astype(BF16)
    else:
        (x_ref,) = refs
    ubuf[0:t, :] = (_rms(x_ref[...]) * gs + shift).astype(BF16)
    return x_ref


def _edge_flags(tiles_per_seq):
    i = pl.program_id(0)
    pos = lax.rem(i, tiles_per_seq)
    return (pos != 0).astype(F32), (pos != tiles_per_seq - 1).astype(F32)


def _row_masks(t, rows_per_seq):
    pos = lax.rem(lax.broadcasted_iota(jnp.int32, (t, CONV_CHUNK), 0), rows_per_seq)
    return pos == 0, pos == rows_per_seq - 1


def _conv3(h, k, t, has_halo, flags, first_row, last_row):
    zm = h[0:t]
    if has_halo:
        zp = h[t + HALO - 1:t + HALO] * flags[0]
        zn = h[t + HALO:t + HALO + 1] * flags[1]
    else:
        zp = jnp.zeros((1, h.shape[1]), F32)
        zn = zp
    z_prev = jnp.where(first_row, zp, pltpu.roll(zm, 1, 0))
    z_next = jnp.where(last_row, zn, pltpu.roll(zm, t - 1, 0))
    return z_prev * k[0:1] + zm * k[1:2] + z_next * k[2:3]


def _finish(x_ref, y, mod_ref, gpost_ref, o_ref, d, gate_off):
    gate = mod_ref[0, :, gate_off * d:(gate_off + 1) * d]
    o_ref[...] = x_ref[...] + (gate * gpost_ref[...]) * _rms(y)


def _mixer_kernel(*refs, t, d, has_halo, tiles_per_seq, rows_per_seq):
    n_x = 3 if has_halo else 1
    x_refs = refs[:n_x]
    mod_ref, gpre_ref, gpost_ref, win_ref, ck_ref, wout_ref, o_ref, ubuf, ybuf = refs[n_x:]
    x_ref = _fill_u(x_refs, has_halo, mod_ref, gpre_ref, ubuf, t, d, 0)
    flags = _edge_flags(tiles_per_seq) if has_halo else None
    first_row, last_row = _row_masks(t, rows_per_seq)
    for j in range(d // CONV_CHUNK):
        c0 = j * CONV_CHUNK
        b_gate = _dot(ubuf[0:t, :], win_ref[:, c0:c0 + CONV_CHUNK])
        c_gate = _dot(ubuf[...], win_ref[:, d + c0:d + c0 + CONV_CHUNK])
        xp = _dot(ubuf[...], win_ref[:, 2 * d + c0:2 * d + c0 + CONV_CHUNK])
        conv = _conv3(c_gate * xp, ck_ref[:, c0:c0 + CONV_CHUNK], t, has_halo, flags, first_row, last_row)
        ybuf[:, c0:c0 + CONV_CHUNK] = (b_gate * conv).astype(BF16)
    y = _dot(ybuf[...], wout_ref[...])
    _finish(x_ref, y, mod_ref, gpost_ref, o_ref, d, 2)


def _ffn_kernel(*refs, t, d, d_ff, has_halo, tiles_per_seq, rows_per_seq):
    n_x = 3 if has_halo else 1
    x_refs = refs[:n_x]
    mod_ref, gpre_ref, gpost_ref, wup_ref, ck_ref, wdown_ref, o_ref, ubuf, abuf = refs[n_x:]
    x_ref = _fill_u(x_refs, has_halo, mod_ref, gpre_ref, ubuf, t, d, 3)
    flags = _edge_flags(tiles_per_seq) if has_halo else None
    first_row, last_row = _row_masks(t, rows_per_seq)
    for j in range(d_ff // CONV_CHUNK):
        c0 = j * CONV_CHUNK
        hg = _dot(ubuf[...], wup_ref[:, c0:c0 + CONV_CHUNK])
        hu = _dot(ubuf[...], wup_ref[:, d_ff + c0:d_ff + c0 + CONV_CHUNK])
        g = _conv3(hg, ck_ref[:, c0:c0 + CONV_CHUNK], t, has_halo, flags, first_row, last_row)
        u = _conv3(hu, ck_ref[:, d_ff + c0:d_ff + c0 + CONV_CHUNK], t, has_halo, flags, first_row, last_row)
        act = g * (1.0 / (1.0 + jnp.exp(-g))) * u
        abuf[:, c0:c0 + CONV_CHUNK] = act.astype(BF16)
    y = _dot(abuf[...], wdown_ref[...])
    _finish(x_ref, y, mod_ref, gpost_ref, o_ref, d, 5)


def _conv_sublayer(kernel_fn, name, layer, x, mod, gpre, gpost, w_a, ck, w_b, seq_len, t, casts=()):
    rows, d = x.shape
    assert rows % t == 0 and t % V7X_BF16_ROWS == 0 and (seq_len % t == 0 or t % seq_len == 0)
    tiles_per_seq = max(seq_len // t, 1)
    has_halo = t < seq_len
    per_seq_mod = mod.shape[0] > 1
    assert not per_seq_mod or t <= seq_len
    hb = t // HALO
    last_hb = rows // HALO - 1

    def mod_map(i):
        return ((i // tiles_per_seq) if per_seq_mod else 0, 0, 0)

    x_specs = [pl.BlockSpec((t, d), lambda i: (i, 0))]
    x_args = [x]
    if has_halo:
        x_specs = [pl.BlockSpec((HALO, d), lambda i: (jnp.maximum(i * hb - 1, 0), 0)),
                   x_specs[0],
                   pl.BlockSpec((HALO, d), lambda i: (jnp.minimum((i + 1) * hb, last_hb), 0))]
        x_args = [x, x, x]
    m_rows = t + 2 * HALO if has_halo else t
    body = functools.partial(kernel_fn, t=t, d=d, has_halo=has_halo, tiles_per_seq=tiles_per_seq,
                             rows_per_seq=min(t, seq_len))
    in_specs = x_specs + [
        pl.BlockSpec((1, 1, mod.shape[2]), mod_map),
        _layer_spec(gpre, layer[0]), _layer_spec(gpost, layer[0]),
        _layer_spec(w_a, layer[1]), _layer_spec(ck, layer[2]), _layer_spec(w_b, layer[1]),
    ]
    cast_in, cast_out, cast_shape = _cast_specs(casts, rows // t)
    out = pl.pallas_call(
        _with_side_casts(body, len(in_specs), 1, len(casts)),
        grid=(rows // t,),
        in_specs=in_specs + cast_in,
        out_specs=[pl.BlockSpec((t, d), lambda i: (i, 0))] + cast_out,
        out_shape=[jax.ShapeDtypeStruct((rows, d), F32)] + cast_shape,
        scratch_shapes=[pltpu.VMEM((m_rows, d), BF16), pltpu.VMEM((t, w_b.shape[1]), BF16)],
        compiler_params=pltpu.CompilerParams(dimension_semantics=("arbitrary",),
                                             vmem_limit_bytes=V7X_VMEM_LIMIT_BYTES),
        name=name,
    )(*x_args, mod, gpre, gpost, w_a, ck, w_b, *[a for a, _ in casts])
    return out[0], out[1:]


def _mixer(name, layer, x, mod, gpre, gpost, w_in, ck, w_out, seq_len, t, casts=()):
    return _conv_sublayer(_mixer_kernel, name, layer, x, mod, gpre, gpost, w_in, ck, w_out, seq_len, t, casts)


def _ffn(name, layer, x, mod, gpre, gpost, w_up, ck, w_down, seq_len, t):
    kernel_fn = functools.partial(_ffn_kernel, d_ff=w_down.shape[1])
    return _conv_sublayer(kernel_fn, name, layer, x, mod, gpre, gpost, w_up, ck, w_down, seq_len, t)[0]


def _qkv_kernel(*refs, d, n_heads, rope, emit_f32_kv):
    x_ref, mod_ref, gpre_ref, w_ref, qg_ref, kg_ref = refs[:6]
    refs = refs[6:]
    if rope:
        cos_ref, sina_ref, sinb_ref = refs[:3]
        refs = refs[3:]
    q_ref, k_ref, v_ref = refs[:3]
    n_kv32 = 2 if emit_f32_kv else 0
    kv32_refs = refs[3:3 + n_kv32]
    t = x_ref.shape[0]
    shift = mod_ref[0, :, 0:d]
    scale = mod_ref[0, :, d:2 * d]
    u = (_rms(x_ref[...]) * (gpre_ref[...] * (1.0 + scale)) + shift).astype(BF16)
    q_scale = HEAD_DIM ** -0.5 * LOG2_E
    pair = 2 * HEAD_DIM
    n_pairs = (n_heads + N_KV_HEADS) // 2

    if rope:
        (ubuf,) = refs[3 + n_kv32:]
        ubuf[...] = u
        project = lambda p: _dot(ubuf[...], w_ref[:, p * pair:(p + 1) * pair])
        in_flight = 2
    else:
        qkv = _dot(u, w_ref[...])
        project = lambda p: qkv[:, p * pair:(p + 1) * pair]
        in_flight = n_pairs + 1

    def issue_after(done):
        tile = (slice(0, V7X_BF16_ROWS), slice(0, HEAD_DIM))
        blk = done[0:V7X_BF16_ROWS, :]
        ubuf[tile] = ubuf[tile] + (blk - blk).astype(BF16)

    ones = jnp.ones((HEAD_DIM, HEAD_DIM), BF16)

    def head(hv, gain):
        if rope:
            ms = _dot((hv * hv).astype(BF16), ones) * (1.0 / HEAD_DIM)
            hv = hv * lax.rsqrt(ms + EPS) * gain
            hv = (hv * cos_ref[...]
                  + pltpu.roll(hv, HEAD_DIM - HEAD_DIM // 4, 1) * sina_ref[...]
                  + pltpu.roll(hv, HEAD_DIM // 4, 1) * sinb_ref[...])
        else:
            hv = _rms(hv) * gain
        return hv

    raw = {p: project(p) for p in range(min(in_flight, n_pairs + 1))}
    for p in range(n_pairs):
        first = None
        for idx in (2 * p, 2 * p + 1):
            lanes = slice((idx % 2) * HEAD_DIM, (idx % 2 + 1) * HEAD_DIM)
            if idx < n_heads:
                hv = head(raw[p][:, lanes], qg_ref[...])
                q_ref[:, idx * HEAD_DIM:(idx + 1) * HEAD_DIM] = (hv * q_scale).astype(BF16)
            else:
                hk = idx - n_heads
                hv = head(raw[p][:, lanes], kg_ref[...])
                k_ref[:, hk * HEAD_DIM:(hk + 1) * HEAD_DIM] = hv.astype(BF16)
                if emit_f32_kv:
                    kv32_refs[0][pl.ds(hk, t, stride=N_KV_HEADS), :] = hv
            first = hv if first is None else first
        del raw[p]
        if p + in_flight <= n_pairs:
            issue_after(first)
            raw[p + in_flight] = project(p + in_flight)
    v = raw[n_pairs]
    for hk in range(N_KV_HEADS):
        lanes = slice(hk * HEAD_DIM, (hk + 1) * HEAD_DIM)
        v_ref[:, lanes] = v[:, lanes].astype(BF16)
        if emit_f32_kv:
            kv32_refs[1][pl.ds(hk, t, stride=N_KV_HEADS), :] = v[:, lanes]


def _qkv(name, layer, x, mod, gpre, w_qkv, q_gain, k_gain, seq_len, t, rope_tables=None, emit_f32_kv=False,
         casts=()):
    rows, d = x.shape
    n_heads = d // HEAD_DIM
    kv_w = N_KV_HEADS * HEAD_DIM
    tiles_per_seq = max(seq_len // t, 1)
    per_seq_mod = mod.shape[0] > 1
    rope = rope_tables is not None
    assert not (per_seq_mod or rope) or seq_len % t == 0

    def mod_map(i):
        return ((i // tiles_per_seq) if per_seq_mod else 0, 0, 0)

    in_specs = [
        pl.BlockSpec((t, d), lambda i: (i, 0)),
        pl.BlockSpec((1, 1, mod.shape[2]), mod_map),
        _layer_spec(gpre, layer[0]), _layer_spec(w_qkv, layer[1]),
        _layer_spec(q_gain, layer[1]), _layer_spec(k_gain, layer[1]),
    ]
    args = [x, mod, gpre, w_qkv, q_gain, k_gain]
    if rope:
        in_specs += [pl.BlockSpec((t, HEAD_DIM), lambda i: (i % tiles_per_seq, 0))] * 3
        args += list(rope_tables)
    out_specs = [pl.BlockSpec((t, d), lambda i: (i, 0)),
                 pl.BlockSpec((t, kv_w), lambda i: (i, 0)),
                 pl.BlockSpec((t, kv_w), lambda i: (i, 0))]
    out_shape = [jax.ShapeDtypeStruct((rows, d), BF16),
                 jax.ShapeDtypeStruct((rows, kv_w), BF16),
                 jax.ShapeDtypeStruct((rows, kv_w), BF16)]
    if emit_f32_kv:
        out_specs += [pl.BlockSpec((N_KV_HEADS * t, HEAD_DIM), lambda i: (i, 0))] * 2
        out_shape += [jax.ShapeDtypeStruct((N_KV_HEADS * rows, HEAD_DIM), F32)] * 2
    cast_in, cast_out, cast_shape = _cast_specs(casts, rows // t)
    body = functools.partial(_qkv_kernel, d=d, n_heads=n_heads, rope=rope, emit_f32_kv=emit_f32_kv)
    out = pl.pallas_call(
        _with_side_casts(body, len(in_specs), len(out_specs), len(casts)),
        grid=(rows // t,),
        in_specs=in_specs + cast_in,
        out_specs=out_specs + cast_out,
        out_shape=out_shape + cast_shape,
        scratch_shapes=[pltpu.VMEM((t, d), BF16)] if rope else [],
        compiler_params=pltpu.CompilerParams(dimension_semantics=("arbitrary",),
                                             vmem_limit_bytes=V7X_VMEM_LIMIT_BYTES),
        name=name,
    )(*args, *[a for a, _ in casts])
    return out[:len(out_specs)], out[len(out_specs):]


def _kv_block(ref, kk, b0, kb):
    if ref.shape[1] == HEAD_DIM:
        return ref[pl.ds(N_KV_HEADS * b0 + kk, kb, stride=N_KV_HEADS), :].astype(BF16)
    return ref[b0:b0 + kb, kk * HEAD_DIM:(kk + 1) * HEAD_DIM]


def _attn_kernel(*refs, d, n_heads, n_kv_sets, seqs_per_tile):
    x_ref, q_ref = refs[:2]
    kv_refs = refs[2:2 + 2 * n_kv_sets]
    mod_ref, gpost_ref, wo_ref, o_ref, obuf = refs[2 + 2 * n_kv_sets:]
    tq = q_ref.shape[0] // seqs_per_tile
    group = n_heads // N_KV_HEADS
    for sub, kk in [(a, b) for a in range(seqs_per_tile) for b in range(N_KV_HEADS)]:
        rows = slice(sub * tq, (sub + 1) * tq)
        qs = jnp.concatenate(
            [q_ref[rows, (kk * group + g) * HEAD_DIM:(kk * group + g + 1) * HEAD_DIM] for g in range(group)], axis=0)
        m = denom = acc = None
        for s in range(n_kv_sets):
            k_ref, v_ref = kv_refs[2 * s], kv_refs[2 * s + 1]
            n_keys = k_ref.shape[0] * k_ref.shape[1] // (N_KV_HEADS * HEAD_DIM * seqs_per_tile)
            kb = min(KEY_BLOCK, n_keys)
            for b0 in range(sub * n_keys, (sub + 1) * n_keys, kb):
                sc = _dot_nt(qs, _kv_block(k_ref, kk, b0, kb))
                v_blk = _kv_block(v_ref, kk, b0, kb)
                mb = sc.max(axis=-1, keepdims=True)
                if m is None:
                    m = mb
                    p = jnp.exp2(sc - m)
                    denom = p.sum(axis=-1, keepdims=True)
                    acc = _dot(p.astype(BF16), v_blk)
                else:
                    m_new = jnp.maximum(m, mb)
                    alpha = jnp.exp2(m - m_new)
                    p = jnp.exp2(sc - m_new)
                    denom = alpha * denom + p.sum(axis=-1, keepdims=True)
                    acc = alpha * acc + _dot(p.astype(BF16), v_blk)
                    m = m_new
        out = acc * (1.0 / denom)
        for g in range(group):
            hq = kk * group + g
            obuf[rows, hq * HEAD_DIM:(hq + 1) * HEAD_DIM] = out[g * tq:(g + 1) * tq].astype(BF16)
    y = _dot(obuf[...], wo_ref[...])
    _finish(x_ref, y, mod_ref, gpost_ref, o_ref, d, 2)


def _attention(name, layer, x, q, kv_sets, mod, gpost, w_o, seq_len, tq, casts=()):
    rows, d = x.shape
    n_heads = d // HEAD_DIM
    kv_w = N_KV_HEADS * HEAD_DIM
    assert seq_len % tq == 0 or tq % seq_len == 0
    tiles_per_seq = max(seq_len // tq, 1)
    seqs_per_tile = max(tq // seq_len, 1)
    per_seq_mod = mod.shape[0] > 1
    assert not per_seq_mod or seqs_per_tile == 1

    def mod_map(i):
        return ((i // tiles_per_seq) if per_seq_mod else 0, 0, 0)

    in_specs = [pl.BlockSpec((tq, d), lambda i: (i, 0)), pl.BlockSpec((tq, d), lambda i: (i, 0))]
    args = [x, q]
    for k, v, n in kv_sets:
        block = (seqs_per_tile * n * kv_w // k.shape[1], k.shape[1])
        in_specs += [pl.BlockSpec(block, lambda i: (i // tiles_per_seq, 0))] * 2
        args += [k, v]
    in_specs += [pl.BlockSpec((1, 1, mod.shape[2]), mod_map),
                 _layer_spec(gpost, layer[0]), _layer_spec(w_o, layer[1])]
    args += [mod, gpost, w_o]
    cast_in, cast_out, cast_shape = _cast_specs(casts, rows // tq)
    body = functools.partial(_attn_kernel, d=d, n_heads=n_heads, n_kv_sets=len(kv_sets), seqs_per_tile=seqs_per_tile)
    out = pl.pallas_call(
        _with_side_casts(body, len(in_specs), 1, len(casts)),
        grid=(rows // tq,),
        in_specs=in_specs + cast_in,
        out_specs=[pl.BlockSpec((tq, d), lambda i: (i, 0))] + cast_out,
        out_shape=[jax.ShapeDtypeStruct((rows, d), F32)] + cast_shape,
        scratch_shapes=[pltpu.VMEM((tq, d), BF16)],
        compiler_params=pltpu.CompilerParams(dimension_semantics=("arbitrary",),
                                             vmem_limit_bytes=V7X_VMEM_LIMIT_BYTES),
        name=name,
    )(*args, *[a for a, _ in casts])
    return out[0], out[1:]


def _rope_tables(n_lat):
    half = HEAD_DIM // 2
    rows_n = n_lat // GRID_W
    row = np.repeat(np.arange(rows_n), GRID_W).astype(np.float32)
    col = np.tile(np.arange(GRID_W), rows_n).astype(np.float32)
    inv = (np.float32(ROPE_THETA) ** (-np.arange(0, half, 2, dtype=np.float32) / np.float32(half))).astype(np.float32)
    ang_r = row[:, None] * inv[None, :]
    ang_c = col[:, None] * inv[None, :]
    cr, sr, cc, sc = np.cos(ang_r), np.sin(ang_r), np.cos(ang_c), np.sin(ang_c)
    zero = np.zeros_like(sr)
    cos = np.concatenate([cr, cr, cc, cc], axis=-1)
    sin_a = np.concatenate([-sr, zero, -sc, zero], axis=-1)
    sin_b = np.concatenate([zero, sr, zero, sc], axis=-1)
    return jnp.asarray(cos), jnp.asarray(sin_a), jnp.asarray(sin_b)


def kernel(x_prompt, x_sample, cache_k, cache_v, c, c_ctx, mod_w, mod_b, norm_mix_pre, norm_mix_post, norm_ffn_pre, norm_ffn_post, conv_w_in, conv_k, conv_w_out, attn_w_qkv, attn_q_gain, attn_k_gain, attn_w_o, ffn_w_up, ffn_conv, ffn_w_down):
    batch, seq, d = x_prompt.shape
    dec_batch, dec_seq, _ = x_sample.shape
    depth = mod_w.shape[0]
    past_len = cache_k.shape[2]
    t_ctx = 4 * seq
    tq_ctx = 4 * seq
    t_lat = 1024
    tq_lat = 256

    n_cond = dec_batch + 1
    pad = (-n_cond) % V7X_BF16_ROWS
    cvec = jnp.concatenate([c, c_ctx[None, :], jnp.zeros((pad, d), F32)], axis=0)
    mod_all = _modulation(cvec, mod_w, mod_b)

    per_layer_rows = lambda a: a.reshape(a.shape[0], 1, a.shape[1])
    g_mix_pre, g_mix_post = per_layer_rows(norm_mix_pre), per_layer_rows(norm_mix_post)
    g_ffn_pre, g_ffn_post = per_layer_rows(norm_ffn_pre), per_layer_rows(norm_ffn_post)
    q_gain, k_gain = per_layer_rows(attn_q_gain), per_layer_rows(attn_k_gain)
    w_in, w_out = conv_w_in.astype(BF16), conv_w_out.astype(BF16)
    w_qkv, w_o = attn_w_qkv.astype(BF16), attn_w_o.astype(BF16)

    h_ctx = x_prompt.reshape(batch * seq, d)
    h_lat = x_sample.reshape(dec_batch * dec_seq, d)
    rope_tables = _rope_tables(dec_seq)
    new_k, new_v = [], []

    for i in range(depth):
        j = i // 2
        mod_lat = mod_all[i, :dec_batch][:, None, :]
        mod_ctx = mod_all[i, dec_batch:dec_batch + 1][:, None, :]
        if i % 2 == 0:
            h_ctx, (w_down,) = _mixer(f"mixer{i}_ctx", (i, j, j), h_ctx, mod_ctx, g_mix_pre, g_mix_post, w_in, conv_k,
                                      w_out, seq, t_ctx, casts=[(ffn_w_down, i)])
            h_lat, (w_up,) = _mixer(f"mixer{i}_lat", (i, j, j), h_lat, mod_lat, g_mix_pre, g_mix_post, w_in, conv_k,
                                    w_out, dec_seq, t_lat, casts=[(ffn_w_up, i)])
        else:
            (q_c, k_c, v_c, k_c32, v_c32), (w_down,) = _qkv(
                f"qkv{i}_ctx", (i, j), h_ctx, mod_ctx, g_mix_pre, w_qkv, q_gain, k_gain, seq, t_ctx,
                emit_f32_kv=True, casts=[(ffn_w_down, i)])
            new_k.append(k_c32.reshape(batch, seq, N_KV_HEADS, HEAD_DIM))
            new_v.append(v_c32.reshape(batch, seq, N_KV_HEADS, HEAD_DIM))
            h_ctx, (w_up,) = _attention(f"attn{i}_ctx", (i, j), h_ctx, q_c, [(k_c, v_c, seq)], mod_ctx, g_mix_post, w_o,
                                        seq, tq_ctx, casts=[(ffn_w_up, i)])
            (q_l, k_l, v_l), _ = _qkv(f"qkv{i}_lat", (i, j), h_lat, mod_lat, g_mix_pre, w_qkv, q_gain, k_gain,
                                      dec_seq, t_lat, rope_tables=rope_tables)
            ck = cache_k[:, j].reshape(dec_batch * past_len * N_KV_HEADS, HEAD_DIM)
            cv = cache_v[:, j].reshape(dec_batch * past_len * N_KV_HEADS, HEAD_DIM)
            h_lat, _ = _attention(f"attn{i}_lat", (i, j), h_lat, q_l, [(ck, cv, past_len), (k_l, v_l, dec_seq)],
                                  mod_lat, g_mix_post, w_o, dec_seq, tq_lat)
        h_ctx = _ffn(f"ffn{i}_ctx", (i, 0, i), h_ctx, mod_ctx, g_ffn_pre, g_ffn_post, w_up[None], ffn_conv, w_down[None],
                     seq, t_ctx)
        h_lat = _ffn(f"ffn{i}_lat", (i, 0, i), h_lat, mod_lat, g_ffn_pre, g_ffn_post, w_up[None], ffn_conv, w_down[None],
                     dec_seq, t_lat)

    return (h_ctx.reshape(batch, seq, d), h_lat.reshape(dec_batch, dec_seq, d),
            jnp.stack(new_k, axis=1), jnp.stack(new_v, axis=1))
```
